```python
import jax
import jax.numpy as jnp
from jax import lax
import numpy as np

D_MODEL = 1024
BATCH = 8
SEQ = 4096
DEPTH = 4

CTX_LEN = 256
GRID_W = 64

FOURIER_GROUPS = 4
FOURIER_GROUP_DIM = D_MODEL // 8
FOURIER_WIDTH = FOURIER_GROUPS * FOURIER_GROUP_DIM
DN_HEADS = 8
DN_HEAD_DIM = D_MODEL // 8
DN_WIDTH = DN_HEADS * DN_HEAD_DIM
CHUNK = 64
SHORT_CONV = 3
N_DIRS = 2
N_BRANCHES = 2
N_GROUPS = 4
EXPERTS_PER_GROUP = 8
N_EXPERTS = N_GROUPS * EXPERTS_PER_GROUP
TOP_K = 2
EXPERT_DIM = D_MODEL // 2
MOE_BLOCK = 256
N_MOD = 6
EPS = 1e-6

OFF_QKV = 0
OFF_A = OFF_QKV + 3 * DN_WIDTH
OFF_B = OFF_A + N_DIRS * DN_HEADS
OFF_Z = OFF_B + N_DIRS * DN_HEADS
OFF_F = OFF_Z + DN_WIDTH
OFF_G = OFF_F + FOURIER_WIDTH
IN_COLS = OFF_G + N_BRANCHES * D_MODEL

kernel_name = 'hybrid_fourier_gdn_hmoe_diffusion'


def rmsnorm(x, w):
    xf = x.astype(jnp.float32)
    y = xf * lax.rsqrt(jnp.mean(xf * xf, axis=-1, keepdims=True) + EPS)
    return (y * w.astype(jnp.float32)).astype(x.dtype)


def l2norm(x):
    return x * lax.rsqrt(jnp.sum(x * x, axis=-1, keepdims=True) + EPS)


def modulate(h, shift, scale):
    return h * (1 + scale) + shift


def centred_dwconv(x, w):
    taps = w.shape[0]
    half = taps // 2
    t = x.shape[-2]
    xp = jnp.pad(x, [(0, 0)] * (x.ndim - 2) + [(half, half), (0, 0)])
    return sum(w[j] * xp[..., j:j + t, :] for j in range(taps))


def fourier_mix(u):
    b, t, _ = u.shape
    ug = u.astype(jnp.float32).reshape(b, t, FOURIER_GROUPS, FOURIER_GROUP_DIM)
    y = jnp.fft.fftn(ug, axes=(1, 3), norm='ortho').real
    return y.reshape(b, t, FOURIER_WIDTH).astype(u.dtype)


def delta_inputs(p, conv_w, a_log, dt_bias, rows):
    b, t, _ = p.shape
    qkv = p[..., OFF_QKV:OFF_A]
    if rows is None:
        qkv = centred_dwconv(qkv, conv_w)
    else:
        qkv = centred_dwconv(qkv.reshape(b, rows, GRID_W, 3 * DN_WIDTH), conv_w).reshape(b, t, 3 * DN_WIDTH)
    qkv = jax.nn.silu(qkv).astype(jnp.float32)
    q, k, v = (qkv[..., j * DN_WIDTH:(j + 1) * DN_WIDTH].reshape(b, t, DN_HEADS, DN_HEAD_DIM) for j in range(3))
    q = jnp.swapaxes(l2norm(q), 1, 2) * DN_HEAD_DIM ** -0.5
    k = jnp.swapaxes(l2norm(k), 1, 2)
    v = jnp.swapaxes(v, 1, 2)
    a = p[..., OFF_A:OFF_B].astype(jnp.float32).reshape(b, t, N_DIRS, DN_HEADS)
    bg = p[..., OFF_B:OFF_Z].astype(jnp.float32).reshape(b, t, N_DIRS, DN_HEADS)
    g = -jnp.exp(a_log.astype(jnp.float32)) * jax.nn.softplus(a + dt_bias.astype(jnp.float32))
    beta = jax.nn.sigmoid(bg)
    g = jnp.transpose(g, (2, 0, 3, 1))
    beta = jnp.transpose(beta, (2, 0, 3, 1))
    return q, k, v, g, beta


def gated_delta_chunked(q, k, v, g, beta, state0, need_out):
    b, h, t, dk = q.shape
    dv = v.shape[-1]
    n = t // CHUNK
    q, k, v = (u.reshape(b, h, n, CHUNK, u.shape[-1]) for u in (q, k, v))
    beta = beta.reshape(b, h, n, CHUNK)
    gcum = jnp.cumsum(g.reshape(b, h, n, CHUNK), axis=-1)
    tri = jnp.tril(jnp.ones((CHUNK, CHUNK), bool))
    strict = jnp.tril(jnp.ones((CHUNK, CHUNK), bool), -1)
    diff = gcum[..., :, None] - gcum[..., None, :]
    decay = jnp.where(tri, jnp.exp(jnp.where(tri, diff, 0.0)), 0.0)
    kk = jnp.einsum('bhnid,bhnjd->bhnij', k, k)
    lower = jnp.where(strict, beta[..., :, None] * kk * decay, 0.0)
    rhs = jnp.concatenate([v * beta[..., None], k * (beta * jnp.exp(gcum))[..., None]], axis=-1)
    sol = lax.linalg.triangular_solve(lower + jnp.eye(CHUNK, dtype=jnp.float32), rhs,
                                      left_side=True, lower=True, unit_diagonal=True)
    u, w = sol[..., :dv], sol[..., dv:]
    g_last = gcum[..., -1]
    k_tail = k * jnp.exp(g_last[..., None] - gcum)[..., None]
    seq_in = (k_tail, u, w, g_last)
    if need_out:
        q_dec = q * jnp.exp(gcum)[..., None]
        qk = jnp.einsum('bhnid,bhnjd->bhnij', q, k) * decay
        seq_in = seq_in + (q_dec, qk)
    seq_in = tuple(jnp.moveaxis(s, 2, 0) for s in seq_in)

    def step(state, xs):
        k_c, u_c, w_c, gl_c = xs[:4]
        v_new = u_c - jnp.einsum('bhck,bhkv->bhcv', w_c, state)
        new_state = state * jnp.exp(gl_c)[..., None, None] + jnp.einsum('bhck,bhcv->bhkv', k_c, v_new)
        if not need_out:
            return new_state, None
        q_c, qk_c = xs[4:]
        o = jnp.einsum('bhck,bhkv->bhcv', q_c, state) + jnp.einsum('bhcs,bhsv->bhcv', qk_c, v_new)
        return new_state, o

    final, o = lax.scan(step, state0, seq_in)
    if not need_out:
        return None, final
    return jnp.moveaxis(o, 0, 2).reshape(b, h, t, dv), final


def bidir_gated_delta(ctx_in, lat_in, need_ctx_out):
    qc, kc, vc, gc, bc = ctx_in
    ql, kl, vl, gl, bl = lat_in
    zero = jnp.zeros((ql.shape[0], DN_HEADS, DN_HEAD_DIM, DN_HEAD_DIM), jnp.float32)
    rev = lambda s: jnp.flip(s, axis=2)
    oc_f, s_f = gated_delta_chunked(qc, kc, vc, gc[0], bc[0], zero, need_ctx_out)
    ol_f, _ = gated_delta_chunked(ql, kl, vl, gl[0], bl[0], s_f, True)
    oc_b, s_b = gated_delta_chunked(rev(qc), rev(kc), rev(vc), rev(gc[1]), rev(bc[1]), zero, need_ctx_out)
    ol_b, _ = gated_delta_chunked(rev(ql), rev(kl), rev(vl), rev(gl[1]), rev(bl[1]), s_b, True)
    o_ctx = oc_f + rev(oc_b) if need_ctx_out else None
    return o_ctx, ol_f + rev(ol_b)


def branch_merge(p, o, w_fourier, w_delta, w_out, out_norm):
    b, t, _ = p.shape
    z = p[..., OFF_Z:OFF_F].reshape(b, t, DN_HEADS, DN_HEAD_DIM).astype(jnp.float32)
    o = jnp.swapaxes(o, 1, 2)
    od = (rmsnorm(o, out_norm) * jax.nn.silu(z)).astype(p.dtype).reshape(b, t, DN_WIDTH)
    pa = fourier_mix(p[..., OFF_F:OFF_G]) @ w_fourier
    pb = od @ w_delta
    gates = jax.nn.sigmoid(p[..., OFF_G:])
    return (gates[..., :D_MODEL] * pa + gates[..., D_MODEL:] * pb) @ w_out


def hier_moe(h, w_rg, b_rg, w_re, b_re, w_gate, w_up, w_down):
    n, d = h.shape
    lg = (h @ w_rg).astype(jnp.float32) + b_rg.astype(jnp.float32)
    grp = jnp.argmax(lg, axis=-1).astype(jnp.int32)
    p_grp = jnp.take_along_axis(jax.nn.softmax(lg, axis=-1), grp[:, None], axis=-1)
    le = ((h @ w_re).astype(jnp.float32) + b_re.astype(jnp.float32)).reshape(n, N_GROUPS, EXPERTS_PER_GROUP)
    le = jnp.take_along_axis(le, grp[:, None, None], axis=1)[:, 0]
    top_v, top_i = lax.top_k(le, TOP_K)
    comb = (p_grp * jax.nn.softmax(top_v, axis=-1)).astype(h.dtype)
    eid = (grp[:, None] * EXPERTS_PER_GROUP + top_i).reshape(-1).astype(jnp.int32)
    n_assign = n * TOP_K
    tok = jnp.arange(n_assign, dtype=jnp.int32) // TOP_K
    order = jnp.argsort(eid)
    e_sorted = eid[order]
    counts = jnp.bincount(eid, length=N_EXPERTS).astype(jnp.int32)
    padded = (counts + MOE_BLOCK - 1) // MOE_BLOCK * MOE_BLOCK
    pad_end = jnp.cumsum(padded)
    pad_start = pad_end - padded
    start = jnp.cumsum(counts) - counts
    slot = pad_start[e_sorted] + jnp.arange(n_assign, dtype=jnp.int32) - start[e_sorted]
    n_slots = -(-n_assign // MOE_BLOCK) * MOE_BLOCK + N_EXPERTS * MOE_BLOCK
    n_blocks = n_slots // MOE_BLOCK
    slot_tok = jnp.full((n_slots,), n, jnp.int32).at[slot].set(tok[order])
    h_pad = jnp.concatenate([h, jnp.zeros((1, d), h.dtype)], axis=0)
    xs = h_pad[slot_tok].reshape(n_blocks, MOE_BLOCK, d)
    blk_start = jnp.arange(n_blocks, dtype=jnp.int32) * MOE_BLOCK
    blk_e = jnp.minimum(jnp.searchsorted(pad_end, blk_start, side='right'), N_EXPERTS - 1).astype(jnp.int32)

    def expert_block(args):
        xb, e = args
        return (jax.nn.silu(xb @ w_gate[e]) * (xb @ w_up[e])) @ w_down[e]

    ys = lax.map(expert_block, (xs, blk_e)).reshape(n_slots, d)
    slot_of = jnp.zeros((n_assign,), jnp.int32).at[order].set(slot)
    y = ys[slot_of].reshape(n, TOP_K, d)
    return jnp.einsum('nkd,nk->nd', y, comb)


def setup_inputs(seed: int = 0) -> dict:
    key = jax.random.key(seed)
    ks = jax.random.split(key, 26)
    f32 = jnp.float32
    nrm = lambda k, shape, fan_in: jax.random.normal(k, shape, f32) * fan_in ** -0.5
    dt = jnp.exp(jax.random.uniform(ks[9], (DEPTH, N_DIRS, DN_HEADS), f32, np.log(1e-3), np.log(1e-1)))
    return {
        'x': jax.random.normal(ks[0], (BATCH, SEQ, D_MODEL), f32),
        'c': jax.random.normal(ks[1], (BATCH, D_MODEL), f32),
        'ctx': jax.random.normal(ks[2], (BATCH, CTX_LEN, D_MODEL), f32),
        'c_ctx': jax.random.normal(ks[3], (D_MODEL,), f32),
        'w_mod': 0.02 * jax.random.normal(ks[4], (DEPTH, D_MODEL, N_MOD * D_MODEL), f32),
        'b_mod': 0.02 * jax.random.normal(ks[5], (DEPTH, N_MOD * D_MODEL), f32),
        'norm_mix': 1.0 + 0.05 * jax.random.normal(ks[6], (DEPTH, D_MODEL), f32),
        'norm_ffn': 1.0 + 0.05 * jax.random.normal(ks[7], (DEPTH, D_MODEL), f32),
        'w_in': nrm(ks[8], (DEPTH, D_MODEL, IN_COLS), D_MODEL),
        'conv_w': nrm(ks[10], (DEPTH, SHORT_CONV, 3 * DN_WIDTH), SHORT_CONV),
        'a_log': jnp.log(jax.random.uniform(ks[11], (DEPTH, N_DIRS, DN_HEADS), f32, 1.0, 16.0)),
        'dt_bias': dt + jnp.log(-jnp.expm1(-dt)),
        'out_norm': 1.0 + 0.05 * jax.random.normal(ks[12], (DEPTH, DN_HEAD_DIM), f32),
        'w_fourier': nrm(ks[13], (DEPTH, FOURIER_WIDTH, D_MODEL), FOURIER_WIDTH),
        'w_delta': nrm(ks[14], (DEPTH, DN_WIDTH, D_MODEL), DN_WIDTH),
        'w_out': nrm(ks[15], (DEPTH, D_MODEL, D_MODEL), D_MODEL),
        'w_route_group': nrm(ks[16], (DEPTH, D_MODEL, N_GROUPS), D_MODEL),
        'b_route_group': 0.01 * jax.random.normal(ks[17], (DEPTH, N_GROUPS), f32),
        'w_route_expert': nrm(ks[18], (DEPTH, D_MODEL, N_EXPERTS), D_MODEL),
        'b_route_expert': 0.01 * jax.random.normal(ks[19], (DEPTH, N_EXPERTS), f32),
        'w_gate': nrm(ks[20], (DEPTH, N_EXPERTS, D_MODEL, EXPERT_DIM), D_MODEL),
        'w_up': nrm(ks[21], (DEPTH, N_EXPERTS, D_MODEL, EXPERT_DIM), D_MODEL),
        'w_down': nrm(ks[22], (DEPTH, N_EXPERTS, EXPERT_DIM, D_MODEL), EXPERT_DIM),
        'final_norm': 1.0 + 0.05 * jax.random.normal(ks[23], (D_MODEL,), f32),
    }


def reference(x, c, ctx, c_ctx, w_mod, b_mod, norm_mix, norm_ffn, w_in, conv_w, a_log, dt_bias, out_norm,
              w_fourier, w_delta, w_out, w_route_group, b_route_group, w_route_expert, b_route_expert,
              w_gate, w_up, w_down, final_norm):
    bsz, seq, d = x.shape
    rows = seq // GRID_W
    xl, xc = x, ctx
    s_lat = jax.nn.silu(c)
    s_ctx = jax.nn.silu(c_ctx)
    for i in range(DEPTH):
        need_ctx = i < DEPTH - 1
        sh1_l, sc1_l, g1_l, sh2_l, sc2_l, g2_l = jnp.split(s_lat @ w_mod[i] + b_mod[i], N_MOD, axis=-1)
        sh1_c, sc1_c, g1_c, sh2_c, sc2_c, g2_c = jnp.split(s_ctx @ w_mod[i] + b_mod[i], N_MOD, axis=-1)

        hl = modulate(rmsnorm(xl, norm_mix[i]), sh1_l[:, None], sc1_l[:, None])
        hc = modulate(rmsnorm(xc, norm_mix[i]), sh1_c, sc1_c)
        pl = hl @ w_in[i]
        pc = hc @ (w_in[i] if need_ctx else w_in[i][:, :OFF_Z])
        lat_in = delta_inputs(pl, conv_w[i], a_log[i], dt_bias[i], rows)
        ctx_in = delta_inputs(pc, conv_w[i], a_log[i], dt_bias[i], None)
        o_ctx, o_lat = bidir_gated_delta(ctx_in, lat_in, need_ctx)
        xl = xl + g1_l[:, None] * branch_merge(pl, o_lat, w_fourier[i], w_delta[i], w_out[i], out_norm[i])
        if need_ctx:
            xc = xc + g1_c * branch_merge(pc, o_ctx, w_fourier[i], w_delta[i], w_out[i], out_norm[i])

        hl2 = modulate(rmsnorm(xl, norm_ffn[i]), sh2_l[:, None], sc2_l[:, None]).reshape(-1, d)
        moe_args = (w_route_group[i], b_route_group[i], w_route_expert[i], b_route_expert[i],
                    w_gate[i], w_up[i], w_down[i])
        if need_ctx:
            hc2 = modulate(rmsnorm(xc, norm_ffn[i]), sh2_c, sc2_c).reshape(-1, d)
            n_ctx = hc2.shape[0]
            y = hier_moe(jnp.concatenate([hc2, hl2], axis=0), *moe_args)
            xc = xc + g2_c * y[:n_ctx].reshape(xc.shape)
            yl = y[n_ctx:]
        else:
            yl = hier_moe(hl2, *moe_args)
        xl = xl + g2_l[:, None] * yl.reshape(xl.shape)
    return rmsnorm(xl, final_norm)
```

```python
import functools

import jax
import jax.numpy as jnp
from jax import lax
from jax.experimental import pallas as pl
from jax.experimental.pallas import tpu as pltpu

F32 = jnp.float32
BF16 = jnp.bfloat16

EPS = 1e-6
ROWS = 256
CHUNK = 64
HEADS = 8
HEAD_DIM = 128
PAIR = 2 * HEAD_DIM
N_MOD = 6
N_GROUPS = 4
EXPERTS_PER_GROUP = 8
N_EXPERTS = N_GROUPS * EXPERTS_PER_GROUP
TOP_K = 2
MOE_BLOCK = 256
FOURIER_GROUPS = 4
LANES = 128
VMEM_LIMIT = 56 * 1024 * 1024


def _cparams(*sem):
    return pltpu.CompilerParams(dimension_semantics=sem, vmem_limit_bytes=VMEM_LIMIT)


def _sigmoid(x):
    return 1.0 / (1.0 + jnp.exp(-x))


def _dot(a, b):
    return jnp.dot(a, b, preferred_element_type=F32)


def _dot_nt(a, b):
    return lax.dot_general(a, b, (((1,), (1,)), ((), ())), preferred_element_type=F32)


def _dot_tn(a, b):
    return lax.dot_general(a, b, (((0,), (0,)), ((), ())), preferred_element_type=F32)


def _split3(x):
    h1 = x.astype(BF16)
    r1 = x - h1.astype(F32)
    h2 = r1.astype(BF16)
    r2 = r1 - h2.astype(F32)
    return h1, h2, r2.astype(BF16)


def _dot_x01(x, m01):
    a, b, c = _split3(x)
    return _dot(a, m01) + _dot(b, m01) + _dot(c, m01)


def _dot_01x(m01, x):
    a, b, c = _split3(x)
    return _dot(m01, a) + _dot(m01, b) + _dot(m01, c)


def _iota(shape, dim):
    return lax.broadcasted_iota(jnp.int32, shape, dim)


def _mod_kernel(c_ref, w_ref, b_ref, o_ref):
    c = c_ref[...]
    s = c * _sigmoid(c)
    w = w_ref[0]
    a, b, _ = _split3(s)
    wa, wb, _ = _split3(w)
    o_ref[0] = _dot(a, wa) + _dot(a, wb) + _dot(b, wa) + b_ref[0]


def _modulation(cvec, w_mod, b_mod):
    depth, d, n = w_mod.shape
    tn = 1536
    return pl.pallas_call(
        _mod_kernel,
        grid=(depth, n // tn),
        in_specs=[
            pl.BlockSpec((16, d), lambda i, j: (0, 0)),
            pl.BlockSpec((1, d, tn), lambda i, j: (i, 0, j)),
            pl.BlockSpec((1, 1, tn), lambda i, j: (i, 0, j)),
        ],
        out_specs=pl.BlockSpec((1, 16, tn), lambda i, j: (i, 0, j)),
        out_shape=jax.ShapeDtypeStruct((depth, 16, n), F32),
        compiler_params=_cparams("arbitrary", "arbitrary"),
        name="modulation",
    )(cvec, w_mod, b_mod.reshape(depth, 1, n))


COL_QKV = 0
COL_Z = 3 * HEADS * HEAD_DIM
COL_G = COL_Z + HEADS * HEAD_DIM
COL_FC = COL_G + 2 * 1024
COL_FS = COL_FC + 512
MAIN_COLS = COL_FS + 512
COL_TILE = 512


def _norm_mod(x, nw, shift, scale):
    ms = jnp.mean(x * x, axis=-1, keepdims=True)
    return (x * lax.rsqrt(ms + EPS) * nw) * (1.0 + scale) + shift


def _inproj_kernel(x_ref, mod_ref, nw_ref, w_ref, wab_ref, p_ref, ab_ref):
    m = mod_ref[0]
    h = _norm_mod(x_ref[...], nw_ref[...], m[0:1], m[1:2]).astype(BF16)
    for c in range(MAIN_COLS // COL_TILE):
        cs = slice(c * COL_TILE, (c + 1) * COL_TILE)
        p_ref[:, cs] = _dot(h, w_ref[:, cs]).astype(BF16)
    ab_ref[...] = _dot(h, wab_ref[...])


def _mod_row(tpb, bsz):
    return lambda t: (jnp.where(t % tpb == tpb - 1, bsz, t // tpb), 0, 0)


def _inproj(x2, mod, nw, w_main, w_ab, tpb, bsz):
    n, d = x2.shape
    return pl.pallas_call(
        _inproj_kernel,
        grid=(n // ROWS,),
        in_specs=[
            pl.BlockSpec((ROWS, d), lambda t: (t, 0)),
            pl.BlockSpec((1, N_MOD, d), _mod_row(tpb, bsz)),
            pl.BlockSpec((1, d), lambda t: (0, 0)),
            pl.BlockSpec((d, MAIN_COLS), lambda t: (0, 0)),
            pl.BlockSpec((d, LANES), lambda t: (0, 0)),
        ],
        out_specs=[
            pl.BlockSpec((ROWS, MAIN_COLS), lambda t: (t, 0)),
            pl.BlockSpec((ROWS, LANES), lambda t: (t, 0)),
        ],
        out_shape=[
            jax.ShapeDtypeStruct((n, MAIN_COLS), BF16),
            jax.ShapeDtypeStruct((n, LANES), F32),
        ],
        compiler_params=_cparams("parallel"),
        name="inproj",
    )(x2, mod, nw, w_main, w_ab)


DW = HEADS * HEAD_DIM
DCOLS = HEADS * CHUNK


def _prep_kernel(tpb, qkv_ref, ab_ref, cw_ref, gp_ref,
                 kn_ref, qs_ref, kb_ref, rw_ref, ru_ref, kt_ref, qd_ref, dm_ref, gs_ref):
    t = pl.program_id(0)
    is_ctx = (t % tpb) == (tpb - 1)
    r = _iota((ROWS, 1), 0)
    rp = jnp.where(is_ctx, r, r % CHUNK)
    has_prev = rp != 0
    has_next = rp != jnp.where(is_ctx, ROWS - 1, CHUNK - 1)

    ab = ab_ref[...]
    gp = gp_ref[...]
    xg = ab + gp[1:2]
    softplus = jnp.maximum(xg, 0.0) + jnp.log(1.0 + jnp.exp(-jnp.abs(xg)))
    g = -jnp.exp(gp[0:1]) * softplus
    sig = _sigmoid(ab)

    ri = _iota((ROWS, ROWS), 0)
    ci = _iota((ROWS, ROWS), 1)
    same = (ri // CHUNK) == (ci // CHUNK)
    tri_lo = (same & (ci <= ri)).astype(BF16)
    tri_up = (same & (ci >= ri)).astype(BF16)
    ones_bd = same.astype(BF16)
    lane = _iota((ROWS, LANES), 1)
    gc = jnp.where(lane < HEADS, _dot_01x(tri_lo, g), _dot_01x(tri_up, g))
    tot = _dot_01x(ones_bd, g)
    eg = jnp.exp(gc)
    ekt = jnp.exp(tot - gc)
    gs_ref[...] = jnp.exp(tot)

    er = _iota((LANES, DCOLS), 0)
    ec = _iota((LANES, DCOLS), 1)
    di = _iota((ROWS, DCOLS), 0) % CHUNK
    dj = _iota((ROWS, DCOLS), 1) % CHUNK
    for d in range(2):
        expand = (er - d * HEADS == ec // CHUNK).astype(BF16)
        gcol = _dot_x01(gc, expand)
        grow = _dot_01x(ones_bd, jnp.where(di == dj, gcol, 0.0))
        keep = (dj <= di) if d == 0 else (dj >= di)
        dm_ref[:, d * DCOLS:(d + 1) * DCOLS] = jnp.where(keep, jnp.exp(jnp.where(keep, gcol - grow, 0.0)), 0.0)

    cw = cw_ref[...]

    def conv_silu(col):
        cs = slice(col * HEAD_DIM, (col + 1) * HEAD_DIM)
        x = qkv_ref[:, cs].astype(F32)
        prev = jnp.where(has_prev, pltpu.roll(x, 1, 0), 0.0)
        nxt = jnp.where(has_next, pltpu.roll(x, ROWS - 1, 0), 0.0)
        y = cw[1:2, cs] * x + cw[0:1, cs] * prev + cw[2:3, cs] * nxt
        return y * _sigmoid(y)

    for h in range(HEADS):
        hs = slice(h * HEAD_DIM, (h + 1) * HEAD_DIM)
        q = conv_silu(h)
        k = conv_silu(HEADS + h)
        v = conv_silu(2 * HEADS + h)
        qn = q * lax.rsqrt(jnp.sum(q * q, axis=-1, keepdims=True) + EPS) * (HEAD_DIM ** -0.5)
        kn = k * lax.rsqrt(jnp.sum(k * k, axis=-1, keepdims=True) + EPS)
        kn_ref[:, hs] = kn.astype(BF16)
        qs_ref[:, hs] = qn.astype(BF16)
        for d in range(2):
            c = d * HEADS + h
            beta = sig[:, 2 * HEADS + c:2 * HEADS + c + 1]
            egc = eg[:, c:c + 1]
            kb = kn * beta
            kb_ref[d, :, hs] = kb.astype(BF16)
            rw_ref[d, :, hs] = (kb * egc).astype(BF16)
            ru_ref[d, :, hs] = (v * beta).astype(BF16)
            kt_ref[d, :, hs] = (kn * ekt[:, c:c + 1]).astype(BF16)
            qd_ref[d, :, hs] = (qn * egc).astype(BF16)


def _prep(p_main, ab, conv_w, gate_params, tpb):
    n = p_main.shape[0]
    tile = lambda w: pl.BlockSpec((ROWS, w), lambda t: (t, 0))
    tile2 = pl.BlockSpec((2, ROWS, DW), lambda t: (0, t, 0))
    shared = jax.ShapeDtypeStruct((n, DW), BF16)
    perdir = jax.ShapeDtypeStruct((2, n, DW), BF16)
    return pl.pallas_call(
        functools.partial(_prep_kernel, tpb),
        grid=(n // ROWS,),
        in_specs=[
            tile(3 * DW),
            tile(LANES),
            pl.BlockSpec((3, 3 * DW), lambda t: (0, 0)),
            pl.BlockSpec((8, LANES), lambda t: (0, 0)),
        ],
        out_specs=[tile(DW), tile(DW), tile2, tile2, tile2, tile2, tile2, tile(2 * DCOLS), tile(LANES)],
        out_shape=[shared, shared, perdir, perdir, perdir, perdir, perdir,
                   jax.ShapeDtypeStruct((n, 2 * DCOLS), F32), jax.ShapeDtypeStruct((n, LANES), F32)],
        compiler_params=_cparams("parallel"),
        name="delta_prep",
    )(p_main, ab, conv_w, gate_params)


def _blockdiag_rows(x, width):
    lane_blk = _iota(x.shape, 1) // width
    zero = jnp.zeros_like(x)
    return jnp.concatenate([jnp.where(lane_blk == u, x, zero) for u in range(x.shape[1] // width)], axis=0)


def _delta_direction(d, kn_ref, qs_ref, kb_ref, rw_ref, ru_ref, kt_ref, qd_ref, dm_ref, gs_ref, o_ref, s_ref):
    gs = gs_ref[0:8, :]
    i64 = _iota((CHUNK, PAIR), 0)
    j64 = _iota((CHUNK, PAIR), 1) % CHUNK
    eye4 = (i64 == j64).astype(F32)
    diag2 = _iota((CHUNK, 2 * CHUNK), 0) == _iota((CHUNK, 2 * CHUNK), 1) % CHUNK
    lane_p = _iota((8, PAIR), 1)
    bd_mask = (_iota((PAIR, PAIR), 0) // HEAD_DIM) == (_iota((PAIR, PAIR), 1) // HEAD_DIM)
    for g in range(HEADS // 4):
        neg_l = []
        qk_d = []
        for p in range(2):
            pp = 2 * g + p
            ls = slice(pp * PAIR, (pp + 1) * PAIR)
            lhs = jnp.concatenate([kb_ref[:, ls], qs_ref[:, ls]], axis=0)
            gram = _dot_nt(lhs, _blockdiag_rows(kn_ref[:, ls], HEAD_DIM))
            dm = dm_ref[:, pp * 2 * CHUNK:(pp + 1) * 2 * CHUNK]
            neg_l.append(jnp.where(diag2, 0.0, -gram[:CHUNK] * dm))
            qk_d.append((gram[CHUNK:] * dm).astype(BF16))
        m = jnp.concatenate(neg_l, axis=1)
        t_inv = eye4 + m
        mj = m
        for _ in range(5):
            mjb = mj.astype(BF16)
            mj = _dot(mjb, _blockdiag_rows(mjb, CHUNK))
            t_inv = t_inv + _dot(mj.astype(BF16), _blockdiag_rows(t_inv.astype(BF16), CHUNK))
        rhs = jnp.concatenate(
            [jnp.concatenate([ru_ref[:, (4 * g + u) * HEAD_DIM:(4 * g + u + 1) * HEAD_DIM],
                              rw_ref[:, (4 * g + u) * HEAD_DIM:(4 * g + u + 1) * HEAD_DIM]], axis=1)
             for u in range(4)], axis=0)
        uw = _dot(_blockdiag_rows(t_inv.astype(BF16), CHUNK), rhs)
        for p in range(2):
            pp = 2 * g + p
            ls = slice(pp * PAIR, (pp + 1) * PAIR)
            r0 = slice(2 * p * CHUNK, (2 * p + 1) * CHUNK)
            r1 = slice((2 * p + 1) * CHUNK, (2 * p + 2) * CHUNK)
            u_pair = jnp.concatenate([uw[r0, :HEAD_DIM], uw[r1, :HEAD_DIM]], axis=1)
            w_pair = jnp.concatenate([uw[r0, HEAD_DIM:], uw[r1, HEAD_DIM:]], axis=1).astype(BF16)
            s_old = s_ref[pp]
            wq_s = _dot(jnp.concatenate([w_pair, qd_ref[:, ls]], axis=0), s_old.astype(BF16))
            v_new = u_pair - wq_s[:CHUNK]
            v_new_b = v_new.astype(BF16)
            o_ref[:, ls] = wq_s[CHUNK:] + _dot(qk_d[p], _blockdiag_rows(v_new_b, HEAD_DIM))
            c0 = d * HEADS + 2 * pp
            decay = jnp.where(lane_p < HEAD_DIM, gs[:, c0:c0 + 1], gs[:, c0 + 1:c0 + 2])
            upd = _dot_tn(kt_ref[:, ls], v_new_b)
            s_dec = (s_old.reshape(PAIR // 8, 8, PAIR) * decay[None]).reshape(PAIR, PAIR)
            s_ref[pp] = s_dec + jnp.where(bd_mask, upd, 0.0)


def _scan_kernel(knf, qsf, kbf, rwf, ruf, ktf, qdf, dmf, gsf,
                 knb, qsb, kbb, rwb, rub, ktb, qdb, dmb, gsb,
                 of_ref, ob_ref, sf_ref, sb_ref):
    @pl.when(pl.program_id(1) == 0)
    def _():
        sf_ref[...] = jnp.zeros_like(sf_ref)
        sb_ref[...] = jnp.zeros_like(sb_ref)

    _delta_direction(0, knf, qsf, kbf, rwf, ruf, ktf, qdf, dmf, gsf, of_ref, sf_ref)
    _delta_direction(1, knb, qsb, kbb, rwb, rub, ktb, qdb, dmb, gsb, ob_ref, sb_ref)


def _scan(kn, qs, kb, rw, ru, kt, qd, dm, gs, bsz, n_lat_chunks, n_ctx_chunks):
    nc = n_lat_chunks + n_ctx_chunks
    fwd = lambda j: jnp.where(j < n_ctx_chunks, n_lat_chunks + j, j - n_ctx_chunks)
    bwd = lambda j: nc - 1 - j
    r4 = lambda a: a.reshape(bsz, nc, CHUNK, a.shape[-1])
    r5 = lambda a: a.reshape(2, bsz, nc, CHUNK, a.shape[-1])

    def shared(cmap, w, col=0):
        return pl.BlockSpec((None, None, CHUNK, w), lambda b, j: (b, cmap(j), 0, col))

    def perdir(d, cmap):
        return pl.BlockSpec((None, None, None, CHUNK, DW), lambda b, j: (d, b, cmap(j), 0, 0))

    def specs(d, cmap):
        return [shared(cmap, DW), shared(cmap, DW), perdir(d, cmap), perdir(d, cmap), perdir(d, cmap),
                perdir(d, cmap), perdir(d, cmap), shared(cmap, DCOLS, d), shared(cmap, LANES)]

    args = (r4(kn), r4(qs), r5(kb), r5(rw), r5(ru), r5(kt), r5(qd), r4(dm), r4(gs))
    o_shape = jax.ShapeDtypeStruct((bsz, nc, CHUNK, DW), F32)
    of, ob = pl.pallas_call(
        _scan_kernel,
        grid=(bsz, nc),
        in_specs=specs(0, fwd) + specs(1, bwd),
        out_specs=[shared(fwd, DW), shared(bwd, DW)],
        out_shape=[o_shape, o_shape],
        scratch_shapes=[pltpu.VMEM((HEADS // 2, PAIR, PAIR), F32), pltpu.VMEM((HEADS // 2, PAIR, PAIR), F32)],
        compiler_params=_cparams("parallel", "arbitrary"),
        name="delta_scan",
    )(*args, *args)
    return of.reshape(-1, DW), ob.reshape(-1, DW)


def _fourier_kernel(ct_ref, st_ref, uc_ref, us_ref, o_ref):
    o_ref[...] = (_dot(ct_ref[...], uc_ref[...]) - _dot(st_ref[...], us_ref[...])).astype(BF16)


def _fourier(p3, ct, st):
    bsz, tt, _ = p3.shape
    fw = 512
    return pl.pallas_call(
        _fourier_kernel,
        grid=(bsz, tt // ROWS),
        in_specs=[
            pl.BlockSpec((ROWS, tt), lambda b, i: (i, 0)),
            pl.BlockSpec((ROWS, tt), lambda b, i: (i, 0)),
            pl.BlockSpec((None, tt, fw), lambda b, i: (b, 0, COL_FC // fw)),
            pl.BlockSpec((None, tt, fw), lambda b, i: (b, 0, COL_FS // fw)),
        ],
        out_specs=pl.BlockSpec((None, ROWS, fw), lambda b, i: (b, i, 0)),
        out_shape=jax.ShapeDtypeStruct((bsz, tt, fw), BF16),
        compiler_params=_cparams("parallel", "arbitrary"),
        name="fourier_mix",
    )(ct, st, p3, p3)


def _merge_kernel(of_ref, ob_ref, z_ref, gt_ref, fm_ref, x_ref, mod_ref, on_ref, nf_ref,
                  wf_ref, wd_ref, wo_ref, wr_ref, br_ref, xo_ref, h2_ref, rt_ref):
    m = mod_ref[0]
    o = of_ref[...] + ob_ref[...]
    on = on_ref[...]
    parts = []
    for h in range(HEADS):
        hs = slice(h * HEAD_DIM, (h + 1) * HEAD_DIM)
        oh = o[:, hs]
        z = z_ref[:, hs].astype(F32)
        y = oh * lax.rsqrt(jnp.mean(oh * oh, axis=-1, keepdims=True) + EPS) * on
        parts.append((y * (z * _sigmoid(z))).astype(BF16))
    od = jnp.concatenate(parts, axis=1)
    pa = _dot(fm_ref[...], wf_ref[...])
    pb = _dot(od, wd_ref[...])
    d = pa.shape[1]
    ga = _sigmoid(gt_ref[:, :d].astype(F32))
    gb = _sigmoid(gt_ref[:, d:].astype(F32))
    y = _dot((ga * pa + gb * pb).astype(BF16), wo_ref[...])
    xn = x_ref[...] + m[2:3] * y
    xo_ref[...] = xn
    h2 = _norm_mod(xn, nf_ref[...], m[3:4], m[4:5])
    h2_ref[...] = h2

    a, b, _ = _split3(h2)
    wr = wr_ref[...]
    wa, wb, _ = _split3(wr)
    lg = _dot(a, wa) + _dot(a, wb) + _dot(b, wa) + br_ref[...]
    lane = _iota(lg.shape, 1).astype(F32)
    big = jnp.float32(1 << 20)
    ninf = jnp.float32(-jnp.inf)
    glog = jnp.where(lane < N_GROUPS, lg, ninf)
    gmax = jnp.max(glog, axis=-1, keepdims=True)
    grp = jnp.min(jnp.where(glog == gmax, lane, big), axis=-1, keepdims=True)
    p_grp = 1.0 / jnp.sum(jnp.exp(glog - gmax), axis=-1, keepdims=True)
    lo = N_GROUPS + grp * EXPERTS_PER_GROUP
    el = jnp.where((lane >= lo) & (lane < lo + EXPERTS_PER_GROUP), lg, ninf)
    v1 = jnp.max(el, axis=-1, keepdims=True)
    i1 = jnp.min(jnp.where(el == v1, lane, big), axis=-1, keepdims=True)
    el2 = jnp.where(lane == i1, ninf, el)
    v2 = jnp.max(el2, axis=-1, keepdims=True)
    i2 = jnp.min(jnp.where(el2 == v2, lane, big), axis=-1, keepdims=True)
    e2 = jnp.exp(v2 - v1)
    w1 = p_grp / (1.0 + e2)
    w2 = p_grp * e2 / (1.0 + e2)
    rt = jnp.where(lane == 0, i1 - N_GROUPS,
                   jnp.where(lane == 1, i2 - N_GROUPS,
                             jnp.where(lane == 2, w1, jnp.where(lane == 3, w2, 0.0))))
    rt_ref[...] = rt


def _merge(of, ob, p_main, fm, x2, mod, on_t, nf, wf, wd, wo, wr, br, tpb, bsz):
    n, d = x2.shape
    tile = lambda w, c=0: pl.BlockSpec((ROWS, w), lambda t: (t, c))
    full = lambda a: pl.BlockSpec(a.shape, lambda t: (0,) * a.ndim)
    return pl.pallas_call(
        _merge_kernel,
        grid=(n // ROWS,),
        in_specs=[
            tile(DW), tile(DW),
            tile(DW, COL_Z // DW),
            tile(2 * d, COL_G // (2 * d)),
            tile(fm.shape[1]),
            tile(d),
            pl.BlockSpec((1, N_MOD, d), _mod_row(tpb, bsz)),
            full(on_t), full(nf), full(wf), full(wd), full(wo), full(wr), full(br),
        ],
        out_specs=[tile(d), tile(d), tile(LANES)],
        out_shape=[jax.ShapeDtypeStruct((n, d), F32), jax.ShapeDtypeStruct((n, d), F32),
                   jax.ShapeDtypeStruct((n, LANES), F32)],
        compiler_params=_cparams("parallel"),
        name="merge_route",
    )(of, ob, p_main, p_main, fm, x2, mod, on_t, nf, wf, wd, wo, wr, br)


def _expert_kernel(be_ref, nu_ref, x_ref, wg_ref, wu_ref, wd_ref, y_ref):
    @pl.when(pl.program_id(0) < nu_ref[0])
    def _():
        x = x_ref[...].astype(BF16)
        a = _dot(x, wg_ref[...])
        u = _dot(x, wu_ref[...])
        hmid = (a * _sigmoid(a) * u).astype(BF16)
        y_ref[...] = _dot(hmid, wd_ref[...])

    @pl.when(pl.program_id(0) >= nu_ref[0])
    def _():
        y_ref[...] = jnp.zeros_like(y_ref)


def _experts(blk_e, n_used, xs, wg, wu, wd):
    n_slots, d = xs.shape
    de = wg.shape[-1]
    grid_spec = pltpu.PrefetchScalarGridSpec(
        num_scalar_prefetch=2,
        grid=(n_slots // MOE_BLOCK,),
        in_specs=[
            pl.BlockSpec((MOE_BLOCK, d), lambda i, be, nu: (i, 0)),
            pl.BlockSpec((None, d, de), lambda i, be, nu: (be[i], 0, 0)),
            pl.BlockSpec((None, d, de), lambda i, be, nu: (be[i], 0, 0)),
            pl.BlockSpec((None, de, d), lambda i, be, nu: (be[i], 0, 0)),
        ],
        out_specs=pl.BlockSpec((MOE_BLOCK, d), lambda i, be, nu: (i, 0)),
    )
    return pl.pallas_call(
        _expert_kernel,
        grid_spec=grid_spec,
        out_shape=jax.ShapeDtypeStruct((n_slots, d), F32),
        compiler_params=_cparams("arbitrary"),
        name="moe_experts",
    )(blk_e, n_used, xs, wg, wu, wd)


def _combine_kernel(x_ref, y_ref, rt_ref, mod_ref, o_ref):
    m = mod_ref[0]
    d = x_ref.shape[1]
    rt = rt_ref[...]
    y = rt[:, 2:3] * y_ref[:, :d] + rt[:, 3:4] * y_ref[:, d:]
    o_ref[...] = x_ref[...] + m[5:6] * y


def _combine(x2, y2, rt, mod, tpb, bsz):
    n, d = x2.shape
    tile = lambda w: pl.BlockSpec((ROWS, w), lambda t: (t, 0))
    return pl.pallas_call(
        _combine_kernel,
        grid=(n // ROWS,),
        in_specs=[tile(d), tile(2 * d), tile(LANES), pl.BlockSpec((1, N_MOD, d), _mod_row(tpb, bsz))],
        out_specs=tile(d),
        out_shape=jax.ShapeDtypeStruct((n, d), F32),
        compiler_params=_cparams("parallel"),
        name="moe_combine",
    )(x2, y2, rt, mod)


def _route_tables(rt, n):
    eid = rt[:, :TOP_K].astype(jnp.int32).reshape(-1)
    n_assign = n * TOP_K
    order = jnp.argsort(eid, stable=True).astype(jnp.int32)
    e_sorted = eid[order]
    counts = jnp.zeros((N_EXPERTS,), jnp.int32).at[eid].add(1)
    padded = (counts + MOE_BLOCK - 1) // MOE_BLOCK * MOE_BLOCK
    pad_end = jnp.cumsum(padded)
    pad_start = pad_end - padded
    start = jnp.cumsum(counts) - counts
    slot = pad_start[e_sorted] + jnp.arange(n_assign, dtype=jnp.int32) - start[e_sorted]
    n_slots = -(-n_assign // MOE_BLOCK) * MOE_BLOCK + N_EXPERTS * MOE_BLOCK
    n_blocks = n_slots // MOE_BLOCK
    slot_tok = jnp.zeros((n_slots,), jnp.int32).at[slot].set(order // TOP_K)
    slot_of = jnp.zeros((n_assign,), jnp.int32).at[order].set(slot)
    blk_start = jnp.arange(n_blocks, dtype=jnp.int32) * MOE_BLOCK
    blk_e = jnp.minimum(jnp.searchsorted(pad_end, blk_start, side='right'), N_EXPERTS - 1).astype(jnp.int32)
    n_used = (pad_end[-1] // MOE_BLOCK).astype(jnp.int32).reshape(1)
    return slot_tok, slot_of, blk_e, n_used


def _final_kernel(x_ref, w_ref, o_ref):
    x = x_ref[...]
    o_ref[...] = x * lax.rsqrt(jnp.mean(x * x, axis=-1, keepdims=True) + EPS) * w_ref[...]


def _final_norm(x3, w, seq):
    bsz, _, d = x3.shape
    return pl.pallas_call(
        _final_kernel,
        grid=(bsz, seq // ROWS),
        in_specs=[pl.BlockSpec((None, ROWS, d), lambda b, i: (b, i, 0)), pl.BlockSpec((1, d), lambda b, i: (0, 0))],
        out_specs=pl.BlockSpec((None, ROWS, d), lambda b, i: (b, i, 0)),
        out_shape=jax.ShapeDtypeStruct((bsz, seq, d), F32),
        compiler_params=_cparams("parallel", "parallel"),
        name="final_norm",
    )(x3, w)


def _dft_tables(seq, ctx_len, group_dim):
    tt = seq + ctx_len
    r = jnp.arange(tt, dtype=jnp.int32)[:, None]
    c = jnp.arange(tt, dtype=jnp.int32)[None, :]
    lat = (r < seq) & (c < seq)
    ctx = (r >= seq) & (c >= seq)
    k_lat = (r * c) % seq
    k_ctx = ((r - seq) * (c - seq)) % ctx_len
    ang = jnp.where(lat, k_lat.astype(F32) * (2.0 * jnp.pi / seq), k_ctx.astype(F32) * (2.0 * jnp.pi / ctx_len))
    scale = jnp.where(lat, (seq * group_dim) ** -0.5, jnp.where(ctx, (ctx_len * group_dim) ** -0.5, 0.0))
    return (jnp.cos(ang) * scale).astype(BF16), (jnp.sin(ang) * scale).astype(BF16)


def _channel_dft(group_dim):
    k = (jnp.arange(group_dim, dtype=jnp.int32)[:, None] * jnp.arange(group_dim, dtype=jnp.int32)[None, :]) % group_dim
    ang = k.astype(F32) * (2.0 * jnp.pi / group_dim)
    eye = jnp.eye(FOURIER_GROUPS, dtype=F32)
    return jnp.kron(eye, jnp.cos(ang)), jnp.kron(eye, jnp.sin(ang))


def kernel(x, c, ctx, c_ctx, w_mod, b_mod, norm_mix, norm_ffn, w_in, conv_w, a_log, dt_bias, out_norm,
           w_fourier, w_delta, w_out, w_route_group, b_route_group, w_route_expert, b_route_expert,
           w_gate, w_up, w_down, final_norm):
    bsz, seq, d = x.shape
    ctx_len = ctx.shape[1]
    depth = w_mod.shape[0]
    assert d == HEADS * HEAD_DIM and ctx_len == ROWS and seq % ROWS == 0 and bsz < 16
    tt = seq + ctx_len
    tpb = tt // ROWS
    n = bsz * tt
    fw = w_fourier.shape[1]
    group_dim = fw // FOURIER_GROUPS

    cvec = jnp.zeros((16, d), F32).at[:bsz].set(c).at[bsz].set(c_ctx)
    mod_all = _modulation(cvec, w_mod, b_mod).reshape(depth, 16, N_MOD, d)
    ct, st = _dft_tables(seq, ctx_len, group_dim)
    cc, sc = _channel_dft(group_dim)

    o_a = 3 * DW
    o_b = o_a + 2 * HEADS
    o_z = o_b + 2 * HEADS
    o_f = o_z + DW
    o_g = o_f + fw

    xcur = jnp.concatenate([x, ctx], axis=1).reshape(n, d)
    for i in range(depth):
        wi = w_in[i]
        wfold_c = jnp.dot(wi[:, o_f:o_g], cc, precision=lax.Precision.HIGHEST)
        wfold_s = jnp.dot(wi[:, o_f:o_g], sc, precision=lax.Precision.HIGHEST)
        w_main = jnp.concatenate([wi[:, :o_a], wi[:, o_z:o_f], wi[:, o_g:], wfold_c, wfold_s], axis=1).astype(BF16)
        w_ab = jnp.concatenate([wi[:, o_a:o_z], jnp.zeros((d, LANES - 4 * HEADS), F32)], axis=1).astype(BF16)
        gate_params = jnp.zeros((8, LANES), F32)
        gate_params = gate_params.at[0, :2 * HEADS].set(a_log[i].reshape(-1)).at[1, :2 * HEADS].set(dt_bias[i].reshape(-1))
        mod = mod_all[i]

        p_main, ab = _inproj(xcur, mod, norm_mix[i].reshape(1, d), w_main, w_ab, tpb, bsz)
        kn, qs, kb, rw, ru, kt, qd, dm, gs = _prep(p_main, ab, conv_w[i], gate_params, tpb)
        of, ob = _scan(kn, qs, kb, rw, ru, kt, qd, dm, gs, bsz, seq // CHUNK, ctx_len // CHUNK)
        fm = _fourier(p_main.reshape(bsz, tt, MAIN_COLS), ct, st).reshape(n, fw)

        wr = jnp.zeros((d, LANES), F32).at[:, :N_GROUPS].set(w_route_group[i])
        wr = wr.at[:, N_GROUPS:N_GROUPS + N_EXPERTS].set(w_route_expert[i])
        br = jnp.zeros((1, LANES), F32).at[0, :N_GROUPS].set(b_route_group[i])
        br = br.at[0, N_GROUPS:N_GROUPS + N_EXPERTS].set(b_route_expert[i])
        xcur, h2, rt = _merge(of, ob, p_main, fm, xcur, mod, out_norm[i].reshape(1, HEAD_DIM),
                              norm_ffn[i].reshape(1, d), w_fourier[i].astype(BF16), w_delta[i].astype(BF16),
                              w_out[i].astype(BF16), wr, br, tpb, bsz)

        slot_tok, slot_of, blk_e, n_used = _route_tables(rt, n)
        ys = _experts(blk_e, n_used, h2[slot_tok], w_gate[i].astype(BF16), w_up[i].astype(BF16),
                      w_down[i].astype(BF16))
        y2 = ys[slot_of].reshape(n, TOP_K * d)
        xcur = _combine(xcur, y2, rt, mod, tpb, bsz)

    return _final_norm(xcur.reshape(bsz, tt, d), final_norm.reshape(1, d), seq)
```

```python
import functools

import jax
import jax.numpy as jnp
from jax import lax
from jax.experimental import pallas as pl
from jax.experimental.pallas import tpu as pltpu

F32 = jnp.float32
BF16 = jnp.bfloat16

EPS = 1e-6
ROWS = 256
CHUNK = 64
HEADS = 8
HEAD_DIM = 128
PAIR = 2 * HEAD_DIM
N_MOD = 6
N_GROUPS = 4
EXPERTS_PER_GROUP = 8
N_EXPERTS = N_GROUPS * EXPERTS_PER_GROUP
TOP_K = 2
MOE_BLOCK = 256
FOURIER_GROUPS = 4
LANES = 128
VMEM_LIMIT = 56 * 1024 * 1024


def _cparams(*sem):
    return pltpu.CompilerParams(dimension_semantics=sem, vmem_limit_bytes=VMEM_LIMIT)


def _sigmoid(x):
    return 1.0 / (1.0 + jnp.exp(-x))


def _dot(a, b):
    return jnp.dot(a, b, preferred_element_type=F32)


def _dot_nt(a, b):
    return lax.dot_general(a, b, (((1,), (1,)), ((), ())), preferred_element_type=F32)


def _dot_tn(a, b):
    return lax.dot_general(a, b, (((0,), (0,)), ((), ())), preferred_element_type=F32)


def _split3(x):
    h1 = x.astype(BF16)
    r1 = x - h1.astype(F32)
    h2 = r1.astype(BF16)
    r2 = r1 - h2.astype(F32)
    return h1, h2, r2.astype(BF16)


def _dot_x01(x, m01):
    a, b, c = _split3(x)
    return _dot(a, m01) + _dot(b, m01) + _dot(c, m01)


def _dot_01x(m01, x):
    a, b, c = _split3(x)
    return _dot(m01, a) + _dot(m01, b) + _dot(m01, c)


def _iota(shape, dim):
    return lax.broadcasted_iota(jnp.int32, shape, dim)


def _mod_kernel(c_ref, w_ref, b_ref, o_ref):
    c = c_ref[...]
    s = c * _sigmoid(c)
    w = w_ref[0]
    a, b, _ = _split3(s)
    wa, wb, _ = _split3(w)
    o_ref[0] = _dot(a, wa) + _dot(a, wb) + _dot(b, wa) + b_ref[0]


def _modulation(cvec, w_mod, b_mod):
    depth, d, n = w_mod.shape
    tn = 1536
    return pl.pallas_call(
        _mod_kernel,
        grid=(depth, n // tn),
        in_specs=[
            pl.BlockSpec((16, d), lambda i, j: (0, 0)),
            pl.BlockSpec((1, d, tn), lambda i, j: (i, 0, j)),
            pl.BlockSpec((1, 1, tn), lambda i, j: (i, 0, j)),
        ],
        out_specs=pl.BlockSpec((1, 16, tn), lambda i, j: (i, 0, j)),
        out_shape=jax.ShapeDtypeStruct((depth, 16, n), F32),
        compiler_params=_cparams("arbitrary", "arbitrary"),
        name="modulation",
    )(cvec, w_mod, b_mod.reshape(depth, 1, n))


COL_QKV = 0
COL_Z = 3 * HEADS * HEAD_DIM
COL_G = COL_Z + HEADS * HEAD_DIM
COL_FC = COL_G + 2 * 1024
COL_FS = COL_FC + 512
MAIN_COLS = COL_FS + 512
COL_TILE = 512


def _norm_mod(x, nw, shift, scale):
    ms = jnp.mean(x * x, axis=-1, keepdims=True)
    return (x * lax.rsqrt(ms + EPS) * nw) * (1.0 + scale) + shift


def _inproj_kernel(x_ref, mod_ref, nw_ref, w_ref, wab_ref, p_ref, ab_ref):
    m = mod_ref[0]
    h = _norm_mod(x_ref[...], nw_ref[...], m[0:1], m[1:2]).astype(BF16)
    for c in range(MAIN_COLS // COL_TILE):
        cs = slice(c * COL_TILE, (c + 1) * COL_TILE)
        p_ref[:, cs] = _dot(h, w_ref[:, cs]).astype(BF16)
    ab_ref[...] = _dot(h, wab_ref[...])


def _mod_row(tpb, bsz):
    return lambda t: (jnp.where(t % tpb == tpb - 1, bsz, t // tpb), 0, 0)


def _inproj(x2, mod, nw, w_main, w_ab, tpb, bsz):
    n, d = x2.shape
    return pl.pallas_call(
        _inproj_kernel,
        grid=(n // ROWS,),
        in_specs=[
            pl.BlockSpec((ROWS, d), lambda t: (t, 0)),
            pl.BlockSpec((1, N_MOD, d), _mod_row(tpb, bsz)),
            pl.BlockSpec((1, d), lambda t: (0, 0)),
            pl.BlockSpec((d, MAIN_COLS), lambda t: (0, 0)),
            pl.BlockSpec((d, LANES), lambda t: (0, 0)),
        ],
        out_specs=[
            pl.BlockSpec((ROWS, MAIN_COLS), lambda t: (t, 0)),
            pl.BlockSpec((ROWS, LANES), lambda t: (t, 0)),
        ],
        out_shape=[
            jax.ShapeDtypeStruct((n, MAIN_COLS), BF16),
            jax.ShapeDtypeStruct((n, LANES), F32),
        ],
        compiler_params=_cparams("parallel"),
        name="inproj",
    )(x2, mod, nw, w_main, w_ab)


DW = HEADS * HEAD_DIM
DCOLS = HEADS * CHUNK


def _blockdiag_rows(x, width):
    lane_blk = _iota(x.shape, 1) // width
    zero = jnp.zeros_like(x)
    return jnp.concatenate([jnp.where(lane_blk == u, x, zero) for u in range(x.shape[1] // width)], axis=0)


def _prep_kernel(tpb, qkv_ref, ab_ref, cw_ref, gp_ref,
                 u_ref, w_ref, kt_ref, qd_ref, qkd_ref, gs_ref,
                 kn_ref, qs_ref, kb_ref, rw_ref, ru_ref, dm_ref):
    t = pl.program_id(0)
    is_ctx = (t % tpb) == (tpb - 1)
    r = _iota((ROWS, 1), 0)
    rp = jnp.where(is_ctx, r, r % CHUNK)
    has_prev = rp != 0
    has_next = rp != jnp.where(is_ctx, ROWS - 1, CHUNK - 1)

    ab = ab_ref[...]
    gp = gp_ref[...]
    xg = ab + gp[1:2]
    softplus = jnp.maximum(xg, 0.0) + jnp.log(1.0 + jnp.exp(-jnp.abs(xg)))
    g = -jnp.exp(gp[0:1]) * softplus
    sig = _sigmoid(ab)

    ri = _iota((ROWS, ROWS), 0)
    ci = _iota((ROWS, ROWS), 1)
    same = (ri // CHUNK) == (ci // CHUNK)
    tri_lo = (same & (ci <= ri)).astype(BF16)
    tri_up = (same & (ci >= ri)).astype(BF16)
    ones_bd = same.astype(BF16)
    lane = _iota((ROWS, LANES), 1)
    gc = jnp.where(lane < HEADS, _dot_01x(tri_lo, g), _dot_01x(tri_up, g))
    tot = _dot_01x(ones_bd, g)
    eg = jnp.exp(gc)
    ekt = jnp.exp(tot - gc)
    gs_ref[...] = jnp.exp(tot)

    er = _iota((LANES, DCOLS), 0)
    ec = _iota((LANES, DCOLS), 1)
    di = _iota((ROWS, DCOLS), 0) % CHUNK
    dj = _iota((ROWS, DCOLS), 1) % CHUNK
    for d in range(2):
        expand = (er - d * HEADS == ec // CHUNK).astype(BF16)
        gcol = _dot_x01(gc, expand)
        grow = _dot_01x(ones_bd, jnp.where(di == dj, gcol, 0.0))
        keep = (dj <= di) if d == 0 else (dj >= di)
        dm_ref[:, d * DCOLS:(d + 1) * DCOLS] = jnp.where(keep, jnp.exp(jnp.where(keep, gcol - grow, 0.0)), 0.0)

    cw = cw_ref[...]

    def conv_silu(col):
        cs = slice(col * HEAD_DIM, (col + 1) * HEAD_DIM)
        x = qkv_ref[:, cs].astype(F32)
        prev = jnp.where(has_prev, pltpu.roll(x, 1, 0), 0.0)
        nxt = jnp.where(has_next, pltpu.roll(x, ROWS - 1, 0), 0.0)
        y = cw[1:2, cs] * x + cw[0:1, cs] * prev + cw[2:3, cs] * nxt
        return y * _sigmoid(y)

    for h in range(HEADS):
        hs = slice(h * HEAD_DIM, (h + 1) * HEAD_DIM)
        q = conv_silu(h)
        k = conv_silu(HEADS + h)
        v = conv_silu(2 * HEADS + h)
        qn = q * lax.rsqrt(jnp.sum(q * q, axis=-1, keepdims=True) + EPS) * (HEAD_DIM ** -0.5)
        kn = k * lax.rsqrt(jnp.sum(k * k, axis=-1, keepdims=True) + EPS)
        kn_ref[:, hs] = kn.astype(BF16)
        qs_ref[:, hs] = qn.astype(BF16)
        for d in range(2):
            c = d * HEADS + h
            beta = sig[:, 2 * HEADS + c:2 * HEADS + c + 1]
            egc = eg[:, c:c + 1]
            kb = kn * beta
            kb_ref[d, :, hs] = kb.astype(BF16)
            rw_ref[d, :, hs] = (kb * egc).astype(BF16)
            ru_ref[d, :, hs] = (v * beta).astype(BF16)
            kt_ref[d, :, hs] = (kn * ekt[:, c:c + 1]).astype(BF16)
            qd_ref[d, :, hs] = (qn * egc).astype(BF16)

    i64 = _iota((CHUNK, PAIR), 0)
    j64 = _iota((CHUNK, PAIR), 1) % CHUNK
    eye4 = (i64 == j64).astype(F32)
    diag2 = _iota((CHUNK, 2 * CHUNK), 0) == _iota((CHUNK, 2 * CHUNK), 1) % CHUNK
    BASE = 4
    blk = {}
    s = BASE
    while s <= CHUNK:
        blk[s] = (i64 // s) == (j64 // s)
        s *= 2

    def solve_chunk(ck, carry):
        rows = pl.ds(pl.multiple_of(ck * CHUNK, CHUNK), CHUNK)
        for d in range(2):
            for g in range(HEADS // 4):
                neg_l = []
                for p in range(2):
                    pp = 2 * g + p
                    ls = slice(pp * PAIR, (pp + 1) * PAIR)
                    lhs = jnp.concatenate([kb_ref[d, rows, ls], qs_ref[rows, ls]], axis=0)
                    gram = _dot_nt(lhs, _blockdiag_rows(kn_ref[rows, ls], HEAD_DIM))
                    dm = dm_ref[rows, d * DCOLS + pp * 2 * CHUNK:d * DCOLS + (pp + 1) * 2 * CHUNK]
                    neg_l.append(jnp.where(diag2, 0.0, -gram[:CHUNK] * dm))
                    qkd_ref[d, rows, pp * 2 * CHUNK:(pp + 1) * 2 * CHUNK] = (gram[CHUNK:] * dm).astype(BF16)
                m = jnp.concatenate(neg_l, axis=1)
                m4 = jnp.where(blk[BASE], m, 0.0)
                m4b = m4.astype(BF16)
                p1 = eye4 + m4
                m4sq = _dot(m4b, _blockdiag_rows(m4b, CHUNK))
                t_inv = p1 + _dot(p1.astype(BF16), _blockdiag_rows(m4sq.astype(BF16), CHUNK))
                s = BASE
                while s < CHUNK:
                    off = jnp.where(blk[2 * s] & ~blk[s], m, 0.0).astype(BF16)
                    tb = t_inv.astype(BF16)
                    ct = _dot(off, _blockdiag_rows(tb, CHUNK))
                    t_inv = t_inv + _dot(tb, _blockdiag_rows(ct.astype(BF16), CHUNK))
                    s *= 2
                rhs = jnp.concatenate(
                    [jnp.concatenate([ru_ref[d, rows, (4 * g + u) * HEAD_DIM:(4 * g + u + 1) * HEAD_DIM],
                                      rw_ref[d, rows, (4 * g + u) * HEAD_DIM:(4 * g + u + 1) * HEAD_DIM]], axis=1)
                     for u in range(4)], axis=0)
                uw = _dot(_blockdiag_rows(t_inv.astype(BF16), CHUNK), rhs)
                for u in range(4):
                    hs = slice((4 * g + u) * HEAD_DIM, (4 * g + u + 1) * HEAD_DIM)
                    u_ref[d, rows, hs] = uw[u * CHUNK:(u + 1) * CHUNK, :HEAD_DIM].astype(BF16)
                    w_ref[d, rows, hs] = uw[u * CHUNK:(u + 1) * CHUNK, HEAD_DIM:].astype(BF16)
        return carry

    lax.fori_loop(0, ROWS // CHUNK, solve_chunk, 0)


def _prep(p_main, ab, conv_w, gate_params, tpb):
    n = p_main.shape[0]
    tile = lambda w: pl.BlockSpec((ROWS, w), lambda t: (t, 0))
    tile2 = lambda w: pl.BlockSpec((2, ROWS, w), lambda t: (0, t, 0))
    perdir = jax.ShapeDtypeStruct((2, n, DW), BF16)
    return pl.pallas_call(
        functools.partial(_prep_kernel, tpb),
        grid=(n // ROWS,),
        in_specs=[
            tile(3 * DW),
            tile(LANES),
            pl.BlockSpec((3, 3 * DW), lambda t: (0, 0)),
            pl.BlockSpec((8, LANES), lambda t: (0, 0)),
        ],
        out_specs=[tile2(DW), tile2(DW), tile2(DW), tile2(DW), tile2(DCOLS), tile(LANES)],
        out_shape=[perdir, perdir, perdir, perdir, jax.ShapeDtypeStruct((2, n, DCOLS), BF16),
                   jax.ShapeDtypeStruct((n, LANES), F32)],
        scratch_shapes=[pltpu.VMEM((ROWS, DW), BF16), pltpu.VMEM((ROWS, DW), BF16),
                        pltpu.VMEM((2, ROWS, DW), BF16), pltpu.VMEM((2, ROWS, DW), BF16),
                        pltpu.VMEM((2, ROWS, DW), BF16), pltpu.VMEM((ROWS, 2 * DCOLS), F32)],
        compiler_params=_cparams("parallel"),
        name="delta_prep",
    )(p_main, ab, conv_w, gate_params)


def _delta_direction(d, u_ref, w_ref, kt_ref, qd_ref, qkd_ref, gs_ref, o_ref, s_ref):
    lane_p = _iota((8, PAIR), 1)
    bd_mask = (_iota((PAIR, PAIR), 0) // HEAD_DIM) == (_iota((PAIR, PAIR), 1) // HEAD_DIM)
    n_chunks = ROWS // CHUNK
    for step in range(n_chunks):
        ck = step if d == 0 else n_chunks - 1 - step
        rows = slice(ck * CHUNK, (ck + 1) * CHUNK)
        gs = gs_ref[ck * CHUNK:ck * CHUNK + 8, :]
        for pp in range(HEADS // 2):
            ls = slice(pp * PAIR, (pp + 1) * PAIR)
            s_old = s_ref[pp]
            wq_s = _dot(jnp.concatenate([w_ref[rows, ls], qd_ref[rows, ls]], axis=0), s_old.astype(BF16))
            v_new = u_ref[rows, ls].astype(F32) - wq_s[:CHUNK]
            v_new_b = v_new.astype(BF16)
            qk = qkd_ref[rows, pp * 2 * CHUNK:(pp + 1) * 2 * CHUNK]
            o_ref[rows, ls] = wq_s[CHUNK:] + _dot(qk, _blockdiag_rows(v_new_b, HEAD_DIM))
            c0 = d * HEADS + 2 * pp
            decay = jnp.where(lane_p < HEAD_DIM, gs[:, c0:c0 + 1], gs[:, c0 + 1:c0 + 2])
            upd = _dot_tn(kt_ref[rows, ls], v_new_b)
            s_dec = (s_old.reshape(PAIR // 8, 8, PAIR) * decay[None]).reshape(PAIR, PAIR)
            s_ref[pp] = s_dec + jnp.where(bd_mask, upd, 0.0)


def _scan_kernel(uf, wf, ktf, qdf, qkf, gsf, ub, wb, ktb, qdb, qkb, gsb, of_ref, ob_ref, sf_ref, sb_ref):
    @pl.when(pl.program_id(1) == 0)
    def _():
        sf_ref[...] = jnp.zeros_like(sf_ref)
        sb_ref[...] = jnp.zeros_like(sb_ref)

    _delta_direction(0, uf, wf, ktf, qdf, qkf, gsf, of_ref, sf_ref)
    _delta_direction(1, ub, wb, ktb, qdb, qkb, gsb, ob_ref, sb_ref)


def _scan(u, w, kt, qd, qkd, gs, bsz, n_lat_blocks, n_ctx_blocks):
    nb = n_lat_blocks + n_ctx_blocks
    fwd = lambda j: jnp.where(j < n_ctx_blocks, n_lat_blocks + j, j - n_ctx_blocks)
    bwd = lambda j: nb - 1 - j
    r5 = lambda a: a.reshape(2, bsz, nb, ROWS, a.shape[-1])

    def perdir(d, cmap, w):
        return pl.BlockSpec((None, None, None, ROWS, w), lambda b, j: (d, b, cmap(j), 0, 0))

    def shared(cmap, w):
        return pl.BlockSpec((None, None, ROWS, w), lambda b, j: (b, cmap(j), 0, 0))

    def specs(d, cmap):
        return [perdir(d, cmap, DW), perdir(d, cmap, DW), perdir(d, cmap, DW), perdir(d, cmap, DW),
                perdir(d, cmap, DCOLS), shared(cmap, LANES)]

    args = (r5(u), r5(w), r5(kt), r5(qd), r5(qkd), gs.reshape(bsz, nb, ROWS, LANES))
    o_shape = jax.ShapeDtypeStruct((bsz, nb, ROWS, DW), F32)
    of, ob = pl.pallas_call(
        _scan_kernel,
        grid=(bsz, nb),
        in_specs=specs(0, fwd) + specs(1, bwd),
        out_specs=[shared(fwd, DW), shared(bwd, DW)],
        out_shape=[o_shape, o_shape],
        scratch_shapes=[pltpu.VMEM((HEADS // 2, PAIR, PAIR), F32), pltpu.VMEM((HEADS // 2, PAIR, PAIR), F32)],
        compiler_params=_cparams("parallel", "arbitrary"),
        name="delta_scan",
    )(*args, *args)
    return of.reshape(-1, DW), ob.reshape(-1, DW)


def _fourier_kernel(ct_ref, st_ref, uc_ref, us_ref, o_ref):
    o_ref[...] = (_dot(ct_ref[...], uc_ref[...]) - _dot(st_ref[...], us_ref[...])).astype(BF16)


def _fourier(p3, ct, st):
    bsz, tt, _ = p3.shape
    fw = 512
    return pl.pallas_call(
        _fourier_kernel,
        grid=(bsz, tt // ROWS),
        in_specs=[
            pl.BlockSpec((ROWS, tt), lambda b, i: (i, 0)),
            pl.BlockSpec((ROWS, tt), lambda b, i: (i, 0)),
            pl.BlockSpec((None, tt, fw), lambda b, i: (b, 0, COL_FC // fw)),
            pl.BlockSpec((None, tt, fw), lambda b, i: (b, 0, COL_FS // fw)),
        ],
        out_specs=pl.BlockSpec((None, ROWS, fw), lambda b, i: (b, i, 0)),
        out_shape=jax.ShapeDtypeStruct((bsz, tt, fw), BF16),
        compiler_params=_cparams("parallel", "arbitrary"),
        name="fourier_mix",
    )(ct, st, p3, p3)


def _merge_kernel(of_ref, ob_ref, z_ref, gt_ref, fm_ref, x_ref, mod_ref, on_ref, nf_ref,
                  wf_ref, wd_ref, wo_ref, wr_ref, br_ref, xo_ref, h2_ref, rt_ref):
    m = mod_ref[0]
    o = of_ref[...] + ob_ref[...]
    on = on_ref[...]
    parts = []
    for h in range(HEADS):
        hs = slice(h * HEAD_DIM, (h + 1) * HEAD_DIM)
        oh = o[:, hs]
        z = z_ref[:, hs].astype(F32)
        y = oh * lax.rsqrt(jnp.mean(oh * oh, axis=-1, keepdims=True) + EPS) * on
        parts.append((y * (z * _sigmoid(z))).astype(BF16))
    od = jnp.concatenate(parts, axis=1)
    pa = _dot(fm_ref[...], wf_ref[...])
    pb = _dot(od, wd_ref[...])
    d = pa.shape[1]
    ga = _sigmoid(gt_ref[:, :d].astype(F32))
    gb = _sigmoid(gt_ref[:, d:].astype(F32))
    y = _dot((ga * pa + gb * pb).astype(BF16), wo_ref[...])
    xn = x_ref[...] + m[2:3] * y
    xo_ref[...] = xn
    h2 = _norm_mod(xn, nf_ref[...], m[3:4], m[4:5])
    h2_ref[...] = h2

    a, b, _ = _split3(h2)
    wr = wr_ref[...]
    wa, wb, _ = _split3(wr)
    lg = _dot(a, wa) + _dot(a, wb) + _dot(b, wa) + br_ref[...]
    lane = _iota(lg.shape, 1).astype(F32)
    big = jnp.float32(1 << 20)
    ninf = jnp.float32(-jnp.inf)
    glog = jnp.where(lane < N_GROUPS, lg, ninf)
    gmax = jnp.max(glog, axis=-1, keepdims=True)
    grp = jnp.min(jnp.where(glog == gmax, lane, big), axis=-1, keepdims=True)
    p_grp = 1.0 / jnp.sum(jnp.exp(glog - gmax), axis=-1, keepdims=True)
    lo = N_GROUPS + grp * EXPERTS_PER_GROUP
    el = jnp.where((lane >= lo) & (lane < lo + EXPERTS_PER_GROUP), lg, ninf)
    v1 = jnp.max(el, axis=-1, keepdims=True)
    i1 = jnp.min(jnp.where(el == v1, lane, big), axis=-1, keepdims=True)
    el2 = jnp.where(lane == i1, ninf, el)
    v2 = jnp.max(el2, axis=-1, keepdims=True)
    i2 = jnp.min(jnp.where(el2 == v2, lane, big), axis=-1, keepdims=True)
    e2 = jnp.exp(v2 - v1)
    w1 = p_grp / (1.0 + e2)
    w2 = p_grp * e2 / (1.0 + e2)
    rt = jnp.where(lane == 0, i1 - N_GROUPS,
                   jnp.where(lane == 1, i2 - N_GROUPS,
                             jnp.where(lane == 2, w1, jnp.where(lane == 3, w2, 0.0))))
    rt_ref[...] = rt


def _merge(of, ob, p_main, fm, x2, mod, on_t, nf, wf, wd, wo, wr, br, tpb, bsz):
    n, d = x2.shape
    tile = lambda w, c=0: pl.BlockSpec((ROWS, w), lambda t: (t, c))
    full = lambda a: pl.BlockSpec(a.shape, lambda t: (0,) * a.ndim)
    return pl.pallas_call(
        _merge_kernel,
        grid=(n // ROWS,),
        in_specs=[
            tile(DW), tile(DW),
            tile(DW, COL_Z // DW),
            tile(2 * d, COL_G // (2 * d)),
            tile(fm.shape[1]),
            tile(d),
            pl.BlockSpec((1, N_MOD, d), _mod_row(tpb, bsz)),
            full(on_t), full(nf), full(wf), full(wd), full(wo), full(wr), full(br),
        ],
        out_specs=[tile(d), tile(d), tile(LANES)],
        out_shape=[jax.ShapeDtypeStruct((n, d), F32), jax.ShapeDtypeStruct((n, d), F32),
                   jax.ShapeDtypeStruct((n, LANES), F32)],
        compiler_params=_cparams("parallel"),
        name="merge_route",
    )(of, ob, p_main, p_main, fm, x2, mod, on_t, nf, wf, wd, wo, wr, br)


def _expert_kernel(be_ref, nu_ref, x_ref, wg_ref, wu_ref, wd_ref, y_ref):
    @pl.when(pl.program_id(0) < nu_ref[0])
    def _():
        x = x_ref[...].astype(BF16)
        a = _dot(x, wg_ref[...])
        u = _dot(x, wu_ref[...])
        hmid = (a * _sigmoid(a) * u).astype(BF16)
        y_ref[...] = _dot(hmid, wd_ref[...])

    @pl.when(pl.program_id(0) >= nu_ref[0])
    def _():
        y_ref[...] = jnp.zeros_like(y_ref)


def _experts(blk_e, n_used, xs, wg, wu, wd):
    n_slots, d = xs.shape
    de = wg.shape[-1]
    grid_spec = pltpu.PrefetchScalarGridSpec(
        num_scalar_prefetch=2,
        grid=(n_slots // MOE_BLOCK,),
        in_specs=[
            pl.BlockSpec((MOE_BLOCK, d), lambda i, be, nu: (i, 0)),
            pl.BlockSpec((None, d, de), lambda i, be, nu: (be[i], 0, 0)),
            pl.BlockSpec((None, d, de), lambda i, be, nu: (be[i], 0, 0)),
            pl.BlockSpec((None, de, d), lambda i, be, nu: (be[i], 0, 0)),
        ],
        out_specs=pl.BlockSpec((MOE_BLOCK, d), lambda i, be, nu: (i, 0)),
    )
    return pl.pallas_call(
        _expert_kernel,
        grid_spec=grid_spec,
        out_shape=jax.ShapeDtypeStruct((n_slots, d), F32),
        compiler_params=_cparams("arbitrary"),
        name="moe_experts",
    )(blk_e, n_used, xs, wg, wu, wd)


def _combine_kernel(x_ref, y0_ref, y1_ref, rt_ref, mod_ref, o_ref):
    m = mod_ref[0]
    rt = rt_ref[...]
    y = rt[:, 2:3] * y0_ref[...] + rt[:, 3:4] * y1_ref[...]
    o_ref[...] = x_ref[...] + m[5:6] * y


def _combine(x2, y0, y1, rt, mod, tpb, bsz):
    n, d = x2.shape
    tile = lambda w: pl.BlockSpec((ROWS, w), lambda t: (t, 0))
    return pl.pallas_call(
        _combine_kernel,
        grid=(n // ROWS,),
        in_specs=[tile(d), tile(d), tile(d), tile(LANES), pl.BlockSpec((1, N_MOD, d), _mod_row(tpb, bsz))],
        out_specs=tile(d),
        out_shape=jax.ShapeDtypeStruct((n, d), F32),
        compiler_params=_cparams("parallel"),
        name="moe_combine",
    )(x2, y0, y1, rt, mod)


def _route_tables(rt, n):
    eid = rt[:, :TOP_K].astype(jnp.int32).reshape(-1)
    n_assign = n * TOP_K
    order = jnp.argsort(eid, stable=True).astype(jnp.int32)
    e_sorted = eid[order]
    counts = jnp.zeros((N_EXPERTS,), jnp.int32).at[eid].add(1)
    padded = (counts + MOE_BLOCK - 1) // MOE_BLOCK * MOE_BLOCK
    pad_end = jnp.cumsum(padded)
    pad_start = pad_end - padded
    start = jnp.cumsum(counts) - counts
    slot = pad_start[e_sorted] + jnp.arange(n_assign, dtype=jnp.int32) - start[e_sorted]
    n_slots = -(-n_assign // MOE_BLOCK) * MOE_BLOCK + N_EXPERTS * MOE_BLOCK
    n_blocks = n_slots // MOE_BLOCK
    slot_tok = jnp.zeros((n_slots,), jnp.int32).at[slot].set(order // TOP_K)
    slot_of = jnp.zeros((n_assign,), jnp.int32).at[order].set(slot)
    blk_start = jnp.arange(n_blocks, dtype=jnp.int32) * MOE_BLOCK
    blk_e = jnp.minimum(jnp.searchsorted(pad_end, blk_start, side='right'), N_EXPERTS - 1).astype(jnp.int32)
    n_used = (pad_end[-1] // MOE_BLOCK).astype(jnp.int32).reshape(1)
    return slot_tok, slot_of, blk_e, n_used


def _final_kernel(x_ref, w_ref, o_ref):
    x = x_ref[...]
    o_ref[...] = x * lax.rsqrt(jnp.mean(x * x, axis=-1, keepdims=True) + EPS) * w_ref[...]


def _final_norm(x3, w, seq):
    bsz, _, d = x3.shape
    return pl.pallas_call(
        _final_kernel,
        grid=(bsz, seq // ROWS),
        in_specs=[pl.BlockSpec((None, ROWS, d), lambda b, i: (b, i, 0)), pl.BlockSpec((1, d), lambda b, i: (0, 0))],
        out_specs=pl.BlockSpec((None, ROWS, d), lambda b, i: (b, i, 0)),
        out_shape=jax.ShapeDtypeStruct((bsz, seq, d), F32),
        compiler_params=_cparams("parallel", "parallel"),
        name="final_norm",
    )(x3, w)


def _dft_tables(seq, ctx_len, group_dim):
    tt = seq + ctx_len
    r = jnp.arange(tt, dtype=jnp.int32)[:, None]
    c = jnp.arange(tt, dtype=jnp.int32)[None, :]
    lat = (r < seq) & (c < seq)
    ctx = (r >= seq) & (c >= seq)
    k_lat = (r * c) % seq
    k_ctx = ((r - seq) * (c - seq)) % ctx_len
    ang = jnp.where(lat, k_lat.astype(F32) * (2.0 * jnp.pi / seq), k_ctx.astype(F32) * (2.0 * jnp.pi / ctx_len))
    scale = jnp.where(lat, (seq * group_dim) ** -0.5, jnp.where(ctx, (ctx_len * group_dim) ** -0.5, 0.0))
    return (jnp.cos(ang) * scale).astype(BF16), (jnp.sin(ang) * scale).astype(BF16)


def _channel_dft(group_dim):
    k = (jnp.arange(group_dim, dtype=jnp.int32)[:, None] * jnp.arange(group_dim, dtype=jnp.int32)[None, :]) % group_dim
    ang = k.astype(F32) * (2.0 * jnp.pi / group_dim)
    eye = jnp.eye(FOURIER_GROUPS, dtype=F32)
    return jnp.kron(eye, jnp.cos(ang)), jnp.kron(eye, jnp.sin(ang))


def kernel(x, c, ctx, c_ctx, w_mod, b_mod, norm_mix, norm_ffn, w_in, conv_w, a_log, dt_bias, out_norm,
           w_fourier, w_delta, w_out, w_route_group, b_route_group, w_route_expert, b_route_expert,
           w_gate, w_up, w_down, final_norm):
    bsz, seq, d = x.shape
    ctx_len = ctx.shape[1]
    depth = w_mod.shape[0]
    assert d == HEADS * HEAD_DIM and ctx_len == ROWS and seq % ROWS == 0 and bsz < 16
    tt = seq + ctx_len
    tpb = tt // ROWS
    n = bsz * tt
    fw = w_fourier.shape[1]
    group_dim = fw // FOURIER_GROUPS

    cvec = jnp.zeros((16, d), F32).at[:bsz].set(c).at[bsz].set(c_ctx)
    mod_all = _modulation(cvec, w_mod, b_mod).reshape(depth, 16, N_MOD, d)
    ct, st = _dft_tables(seq, ctx_len, group_dim)
    cc, sc = _channel_dft(group_dim)

    o_a = 3 * DW
    o_b = o_a + 2 * HEADS
    o_z = o_b + 2 * HEADS
    o_f = o_z + DW
    o_g = o_f + fw

    xcur = jnp.concatenate([x, ctx], axis=1).reshape(n, d)
    for i in range(depth):
        wi = w_in[i]
        wfold_c = jnp.dot(wi[:, o_f:o_g], cc, precision=lax.Precision.HIGHEST)
        wfold_s = jnp.dot(wi[:, o_f:o_g], sc, precision=lax.Precision.HIGHEST)
        w_main = jnp.concatenate([wi[:, :o_a], wi[:, o_z:o_f], wi[:, o_g:], wfold_c, wfold_s], axis=1).astype(BF16)
        w_ab = jnp.concatenate([wi[:, o_a:o_z], jnp.zeros((d, LANES - 4 * HEADS), F32)], axis=1).astype(BF16)
        gate_params = jnp.zeros((8, LANES), F32)
        gate_params = gate_params.at[0, :2 * HEADS].set(a_log[i].reshape(-1)).at[1, :2 * HEADS].set(dt_bias[i].reshape(-1))
        mod = mod_all[i]

        p_main, ab = _inproj(xcur, mod, norm_mix[i].reshape(1, d), w_main, w_ab, tpb, bsz)
        u, w, kt, qd, qkd, gs = _prep(p_main, ab, conv_w[i], gate_params, tpb)
        of, ob = _scan(u, w, kt, qd, qkd, gs, bsz, seq // ROWS, ctx_len // ROWS)
        fm = _fourier(p_main.reshape(bsz, tt, MAIN_COLS), ct, st).reshape(n, fw)

        wr = jnp.zeros((d, LANES), F32).at[:, :N_GROUPS].set(w_route_group[i])
        wr = wr.at[:, N_GROUPS:N_GROUPS + N_EXPERTS].set(w_route_expert[i])
        br = jnp.zeros((1, LANES), F32).at[0, :N_GROUPS].set(b_route_group[i])
        br = br.at[0, N_GROUPS:N_GROUPS + N_EXPERTS].set(b_route_expert[i])
        xcur, h2, rt = _merge(of, ob, p_main, fm, xcur, mod, out_norm[i].reshape(1, HEAD_DIM),
                              norm_ffn[i].reshape(1, d), w_fourier[i].astype(BF16), w_delta[i].astype(BF16),
                              w_out[i].astype(BF16), wr, br, tpb, bsz)

        slot_tok, slot_of, blk_e, n_used = _route_tables(rt, n)
        ys = _experts(blk_e, n_used, h2[slot_tok], w_gate[i].astype(BF16), w_up[i].astype(BF16),
                      w_down[i].astype(BF16))
        xcur = _combine(xcur, ys[slot_of[0::TOP_K]], ys[slot_of[1::TOP_K]], rt, mod, tpb, bsz)

    return _final_norm(xcur.reshape(bsz, tt, d), final_norm.reshape(1, d), seq)
```

```python
import functools

import jax
import jax.numpy as jnp
from jax import lax
from jax.experimental import pallas as pl
from jax.experimental.pallas import tpu as pltpu

F32 = jnp.float32
BF16 = jnp.bfloat16

EPS = 1e-6
ROWS = 256
CHUNK = 64
HEADS = 8
HEAD_DIM = 128
PAIR = 2 * HEAD_DIM
N_MOD = 6
N_GROUPS = 4
EXPERTS_PER_GROUP = 8
N_EXPERTS = N_GROUPS * EXPERTS_PER_GROUP
TOP_K = 2
MOE_BLOCK = 256
FOURIER_GROUPS = 4
LANES = 128
VMEM_LIMIT = 56 * 1024 * 1024


def _cparams(*sem):
    return pltpu.CompilerParams(dimension_semantics=sem, vmem_limit_bytes=VMEM_LIMIT)


def _sigmoid(x):
    return 1.0 / (1.0 + jnp.exp(-x))


def _dot(a, b):
    return jnp.dot(a, b, preferred_element_type=F32)


def _dot_nt(a, b):
    return lax.dot_general(a, b, (((1,), (1,)), ((), ())), preferred_element_type=F32)


def _dot_tn(a, b):
    return lax.dot_general(a, b, (((0,), (0,)), ((), ())), preferred_element_type=F32)


def _split3(x):
    h1 = x.astype(BF16)
    r1 = x - h1.astype(F32)
    h2 = r1.astype(BF16)
    r2 = r1 - h2.astype(F32)
    return h1, h2, r2.astype(BF16)


def _dot_x01(x, m01):
    a, b, c = _split3(x)
    return _dot(a, m01) + _dot(b, m01) + _dot(c, m01)


def _dot_01x(m01, x):
    a, b, c = _split3(x)
    return _dot(m01, a) + _dot(m01, b) + _dot(m01, c)


def _iota(shape, dim):
    return lax.broadcasted_iota(jnp.int32, shape, dim)


def _mod_kernel(c_ref, w_ref, b_ref, o_ref):
    c = c_ref[...]
    s = c * _sigmoid(c)
    w = w_ref[0]
    a, b, _ = _split3(s)
    wa, wb, _ = _split3(w)
    o_ref[0] = _dot(a, wa) + _dot(a, wb) + _dot(b, wa) + b_ref[0]


def _modulation(cvec, w_mod, b_mod):
    depth, d, n = w_mod.shape
    tn = 1536
    return pl.pallas_call(
        _mod_kernel,
        grid=(depth, n // tn),
        in_specs=[
            pl.BlockSpec((16, d), lambda i, j: (0, 0)),
            pl.BlockSpec((1, d, tn), lambda i, j: (i, 0, j)),
            pl.BlockSpec((1, 1, tn), lambda i, j: (i, 0, j)),
        ],
        out_specs=pl.BlockSpec((1, 16, tn), lambda i, j: (i, 0, j)),
        out_shape=jax.ShapeDtypeStruct((depth, 16, n), F32),
        compiler_params=_cparams("arbitrary", "arbitrary"),
        name="modulation",
    )(cvec, w_mod, b_mod.reshape(depth, 1, n))


COL_QKV = 0
COL_Z = 3 * HEADS * HEAD_DIM
COL_G = COL_Z + HEADS * HEAD_DIM
COL_FC = COL_G + 2 * 1024
COL_FS = COL_FC + 512
MAIN_COLS = COL_FS + 512
COL_TILE = 512


def _norm_mod(x, nw, shift, scale):
    ms = jnp.mean(x * x, axis=-1, keepdims=True)
    return (x * lax.rsqrt(ms + EPS) * nw) * (1.0 + scale) + shift


def _inproj_kernel(x_ref, mod_ref, nw_ref, w_ref, wab_ref, p_ref, ab_ref):
    m = mod_ref[0]
    h = _norm_mod(x_ref[...], nw_ref[...], m[0:1], m[1:2]).astype(BF16)
    for c in range(MAIN_COLS // COL_TILE):
        cs = slice(c * COL_TILE, (c + 1) * COL_TILE)
        p_ref[:, cs] = _dot(h, w_ref[:, cs]).astype(BF16)
    ab_ref[...] = _dot(h, wab_ref[...])


def _mod_row(tpb, bsz):
    return lambda t: (jnp.where(t % tpb == tpb - 1, bsz, t // tpb), 0, 0)


def _inproj(x2, mod, nw, w_main, w_ab, tpb, bsz):
    n, d = x2.shape
    return pl.pallas_call(
        _inproj_kernel,
        grid=(n // ROWS,),
        in_specs=[
            pl.BlockSpec((ROWS, d), lambda t: (t, 0)),
            pl.BlockSpec((1, N_MOD, d), _mod_row(tpb, bsz)),
            pl.BlockSpec((1, d), lambda t: (0, 0)),
            pl.BlockSpec((d, MAIN_COLS), lambda t: (0, 0)),
            pl.BlockSpec((d, LANES), lambda t: (0, 0)),
        ],
        out_specs=[
            pl.BlockSpec((ROWS, MAIN_COLS), lambda t: (t, 0)),
            pl.BlockSpec((ROWS, LANES), lambda t: (t, 0)),
        ],
        out_shape=[
            jax.ShapeDtypeStruct((n, MAIN_COLS), BF16),
            jax.ShapeDtypeStruct((n, LANES), F32),
        ],
        compiler_params=_cparams("parallel"),
        name="inproj",
    )(x2, mod, nw, w_main, w_ab)


DW = HEADS * HEAD_DIM
DCOLS = HEADS * CHUNK
SOLVE_CHUNKS = 2


def _blockdiag_rows(x, width):
    lane_blk = _iota(x.shape, 1) // width
    zero = jnp.zeros_like(x)
    return jnp.concatenate([jnp.where(lane_blk == u, x, zero) for u in range(x.shape[1] // width)], axis=0)


def _prep_kernel(tpb, qkv_ref, ab_ref, cw_ref, gp_ref,
                 u_ref, w_ref, kt_ref, qd_ref, qkd_ref, gs_ref,
                 kn_ref, qs_ref, kb_ref, rw_ref, ru_ref, dm_ref):
    t = pl.program_id(0)
    is_ctx = (t % tpb) == (tpb - 1)
    r = _iota((ROWS, 1), 0)
    rp = jnp.where(is_ctx, r, r % CHUNK)
    has_prev = rp != 0
    has_next = rp != jnp.where(is_ctx, ROWS - 1, CHUNK - 1)

    ab = ab_ref[...]
    gp = gp_ref[...]
    xg = ab + gp[1:2]
    softplus = jnp.maximum(xg, 0.0) + jnp.log(1.0 + jnp.exp(-jnp.abs(xg)))
    g = -jnp.exp(gp[0:1]) * softplus
    sig = _sigmoid(ab)

    ri = _iota((ROWS, ROWS), 0)
    ci = _iota((ROWS, ROWS), 1)
    same = (ri // CHUNK) == (ci // CHUNK)
    tri_lo = (same & (ci <= ri)).astype(BF16)
    tri_up = (same & (ci >= ri)).astype(BF16)
    ones_bd = same.astype(BF16)
    lane = _iota((ROWS, LANES), 1)
    gc = jnp.where(lane < HEADS, _dot_01x(tri_lo, g), _dot_01x(tri_up, g))
    tot = _dot_01x(ones_bd, g)
    eg = jnp.exp(gc)
    ekt = jnp.exp(tot - gc)
    gs_ref[...] = jnp.exp(tot)

    er = _iota((LANES, DCOLS), 0)
    ec = _iota((LANES, DCOLS), 1)
    di = _iota((ROWS, DCOLS), 0) % CHUNK
    dj = _iota((ROWS, DCOLS), 1) % CHUNK
    for d in range(2):
        expand = (er - d * HEADS == ec // CHUNK).astype(BF16)
        gcol = _dot_x01(gc, expand)
        grow = _dot_01x(ones_bd, jnp.where(di == dj, gcol, 0.0))
        keep = (dj <= di) if d == 0 else (dj >= di)
        dm_ref[:, d * DCOLS:(d + 1) * DCOLS] = jnp.where(keep, jnp.exp(jnp.where(keep, gcol - grow, 0.0)), 0.0)

    cw = cw_ref[...]

    def conv_silu(col):
        cs = slice(col * HEAD_DIM, (col + 1) * HEAD_DIM)
        x = qkv_ref[:, cs].astype(F32)
        prev = jnp.where(has_prev, pltpu.roll(x, 1, 0), 0.0)
        nxt = jnp.where(has_next, pltpu.roll(x, ROWS - 1, 0), 0.0)
        y = cw[1:2, cs] * x + cw[0:1, cs] * prev + cw[2:3, cs] * nxt
        return y * _sigmoid(y)

    for h in range(HEADS):
        hs = slice(h * HEAD_DIM, (h + 1) * HEAD_DIM)
        q = conv_silu(h)
        k = conv_silu(HEADS + h)
        v = conv_silu(2 * HEADS + h)
        qn = q * lax.rsqrt(jnp.sum(q * q, axis=-1, keepdims=True) + EPS) * (HEAD_DIM ** -0.5)
        kn = k * lax.rsqrt(jnp.sum(k * k, axis=-1, keepdims=True) + EPS)
        kn_ref[:, hs] = kn.astype(BF16)
        qs_ref[:, hs] = qn.astype(BF16)
        for d in range(2):
            c = d * HEADS + h
            beta = sig[:, 2 * HEADS + c:2 * HEADS + c + 1]
            egc = eg[:, c:c + 1]
            kb = kn * beta
            kb_ref[d, :, hs] = kb.astype(BF16)
            rw_ref[d, :, hs] = (kb * egc).astype(BF16)
            ru_ref[d, :, hs] = (v * beta).astype(BF16)
            kt_ref[d, :, hs] = (kn * ekt[:, c:c + 1]).astype(BF16)
            qd_ref[d, :, hs] = (qn * egc).astype(BF16)

    i64 = _iota((CHUNK, PAIR), 0)
    j64 = _iota((CHUNK, PAIR), 1) % CHUNK
    eye4 = (i64 == j64).astype(F32)
    diag2 = _iota((CHUNK, 2 * CHUNK), 0) == _iota((CHUNK, 2 * CHUNK), 1) % CHUNK
    BASE = 4
    blk = {}
    s = BASE
    while s <= CHUNK:
        blk[s] = (i64 // s) == (j64 // s)
        s *= 2

    def solve_chunks(it, carry):
        chains = []
        for cj in range(SOLVE_CHUNKS):
            rows = pl.ds(pl.multiple_of((it * SOLVE_CHUNKS + cj) * CHUNK, CHUNK), CHUNK)
            chains += [(rows, d, g) for d in range(2) for g in range(HEADS // 4)]

        def bd(x):
            return _blockdiag_rows(x.astype(BF16), CHUNK)

        ms = []
        for rows, d, g in chains:
            neg_l = []
            for p in range(2):
                pp = 2 * g + p
                ls = slice(pp * PAIR, (pp + 1) * PAIR)
                lhs = jnp.concatenate([kb_ref[d, rows, ls], qs_ref[rows, ls]], axis=0)
                gram = _dot_nt(lhs, _blockdiag_rows(kn_ref[rows, ls], HEAD_DIM))
                dm = dm_ref[rows, d * DCOLS + pp * 2 * CHUNK:d * DCOLS + (pp + 1) * 2 * CHUNK]
                neg_l.append(jnp.where(diag2, 0.0, -gram[:CHUNK] * dm))
                qkd_ref[d, rows, pp * 2 * CHUNK:(pp + 1) * 2 * CHUNK] = (gram[CHUNK:] * dm).astype(BF16)
            ms.append(jnp.concatenate(neg_l, axis=1))
        m4 = [jnp.where(blk[BASE], m, 0.0) for m in ms]
        m4sq = [_dot(x.astype(BF16), bd(x)) for x in m4]
        t_inv = [(eye4 + x) + _dot((eye4 + x).astype(BF16), bd(y)) for x, y in zip(m4, m4sq)]
        s = BASE
        while s < CHUNK:
            offd = blk[2 * s] & ~blk[s]
            ct = [_dot(jnp.where(offd, m, 0.0).astype(BF16), bd(t)) for m, t in zip(ms, t_inv)]
            t_inv = [t + _dot(t.astype(BF16), bd(c)) for t, c in zip(t_inv, ct)]
            s *= 2
        uws = []
        for (rows, d, g), t in zip(chains, t_inv):
            rhs = jnp.concatenate(
                [jnp.concatenate([ru_ref[d, rows, (4 * g + u) * HEAD_DIM:(4 * g + u + 1) * HEAD_DIM],
                                  rw_ref[d, rows, (4 * g + u) * HEAD_DIM:(4 * g + u + 1) * HEAD_DIM]], axis=1)
                 for u in range(4)], axis=0)
            uws.append(_dot(bd(t), rhs))
        for (rows, d, g), uw in zip(chains, uws):
            for u in range(4):
                hs = slice((4 * g + u) * HEAD_DIM, (4 * g + u + 1) * HEAD_DIM)
                u_ref[d, rows, hs] = uw[u * CHUNK:(u + 1) * CHUNK, :HEAD_DIM].astype(BF16)
                w_ref[d, rows, hs] = uw[u * CHUNK:(u + 1) * CHUNK, HEAD_DIM:].astype(BF16)
        return carry

    lax.fori_loop(0, ROWS // (CHUNK * SOLVE_CHUNKS), solve_chunks, 0)


def _prep(p_main, ab, conv_w, gate_params, tpb):
    n = p_main.shape[0]
    tile = lambda w: pl.BlockSpec((ROWS, w), lambda t: (t, 0))
    tile2 = lambda w: pl.BlockSpec((2, ROWS, w), lambda t: (0, t, 0))
    perdir = jax.ShapeDtypeStruct((2, n, DW), BF16)
    return pl.pallas_call(
        functools.partial(_prep_kernel, tpb),
        grid=(n // ROWS,),
        in_specs=[
            tile(3 * DW),
            tile(LANES),
            pl.BlockSpec((3, 3 * DW), lambda t: (0, 0)),
            pl.BlockSpec((8, LANES), lambda t: (0, 0)),
        ],
        out_specs=[tile2(DW), tile2(DW), tile2(DW), tile2(DW), tile2(DCOLS), tile(LANES)],
        out_shape=[perdir, perdir, perdir, perdir, jax.ShapeDtypeStruct((2, n, DCOLS), BF16),
                   jax.ShapeDtypeStruct((n, LANES), F32)],
        scratch_shapes=[pltpu.VMEM((ROWS, DW), BF16), pltpu.VMEM((ROWS, DW), BF16),
                        pltpu.VMEM((2, ROWS, DW), BF16), pltpu.VMEM((2, ROWS, DW), BF16),
                        pltpu.VMEM((2, ROWS, DW), BF16), pltpu.VMEM((ROWS, 2 * DCOLS), F32)],
        compiler_params=_cparams("parallel"),
        name="delta_prep",
    )(p_main, ab, conv_w, gate_params)


def _scan_kernel(uf, wf, ktf, qdf, qkf, gsf, ub, wb, ktb, qdb, qkb, gsb, of_ref, ob_ref, sf_ref, sb_ref):
    @pl.when(pl.program_id(1) == 0)
    def _():
        sf_ref[...] = jnp.zeros_like(sf_ref)
        sb_ref[...] = jnp.zeros_like(sb_ref)

    dirs = ((uf, wf, ktf, qdf, qkf, gsf, of_ref, sf_ref), (ub, wb, ktb, qdb, qkb, gsb, ob_ref, sb_ref))
    lane_p = _iota((8, PAIR), 1)
    bd_mask = (_iota((PAIR, PAIR), 0) // HEAD_DIM) == (_iota((PAIR, PAIR), 1) // HEAD_DIM)
    n_chunks = ROWS // CHUNK
    chains = [(d, pp) for d in range(2) for pp in range(HEADS // 2)]
    for step in range(n_chunks):
        def rows_of(d):
            ck = step if d == 0 else n_chunks - 1 - step
            return slice(ck * CHUNK, (ck + 1) * CHUNK)

        s_old, wq_s, v_new_b = [], [], []
        for d, pp in chains:
            u_ref, w_ref, kt_ref, qd_ref, qkd_ref, gs_ref, o_ref, s_ref = dirs[d]
            rows, ls = rows_of(d), slice(pp * PAIR, (pp + 1) * PAIR)
            s_old.append(s_ref[pp])
            wq_s.append(_dot(jnp.concatenate([w_ref[rows, ls], qd_ref[rows, ls]], axis=0), s_old[-1].astype(BF16)))
        for i, (d, pp) in enumerate(chains):
            u_ref = dirs[d][0]
            rows, ls = rows_of(d), slice(pp * PAIR, (pp + 1) * PAIR)
            v_new_b.append((u_ref[rows, ls].astype(F32) - wq_s[i][:CHUNK]).astype(BF16))
        for i, (d, pp) in enumerate(chains):
            u_ref, w_ref, kt_ref, qd_ref, qkd_ref, gs_ref, o_ref, s_ref = dirs[d]
            rows, ls = rows_of(d), slice(pp * PAIR, (pp + 1) * PAIR)
            qk = qkd_ref[rows, pp * 2 * CHUNK:(pp + 1) * 2 * CHUNK]
            o_ref[rows, ls] = wq_s[i][CHUNK:] + _dot(qk, _blockdiag_rows(v_new_b[i], HEAD_DIM))
        for i, (d, pp) in enumerate(chains):
            u_ref, w_ref, kt_ref, qd_ref, qkd_ref, gs_ref, o_ref, s_ref = dirs[d]
            rows, ls = rows_of(d), slice(pp * PAIR, (pp + 1) * PAIR)
            gs = gs_ref[rows.start:rows.start + 8, :]
            c0 = d * HEADS + 2 * pp
            decay = jnp.where(lane_p < HEAD_DIM, gs[:, c0:c0 + 1], gs[:, c0 + 1:c0 + 2])
            upd = _dot_tn(kt_ref[rows, ls], v_new_b[i])
            s_dec = (s_old[i].reshape(PAIR // 8, 8, PAIR) * decay[None]).reshape(PAIR, PAIR)
            s_ref[pp] = s_dec + jnp.where(bd_mask, upd, 0.0)


def _scan(u, w, kt, qd, qkd, gs, bsz, n_lat_blocks, n_ctx_blocks):
    nb = n_lat_blocks + n_ctx_blocks
    fwd = lambda j: jnp.where(j < n_ctx_blocks, n_lat_blocks + j, j - n_ctx_blocks)
    bwd = lambda j: nb - 1 - j
    r5 = lambda a: a.reshape(2, bsz, nb, ROWS, a.shape[-1])

    def perdir(d, cmap, w):
        return pl.BlockSpec((None, None, None, ROWS, w), lambda b, j: (d, b, cmap(j), 0, 0))

    def shared(cmap, w):
        return pl.BlockSpec((None, None, ROWS, w), lambda b, j: (b, cmap(j), 0, 0))

    def specs(d, cmap):
        return [perdir(d, cmap, DW), perdir(d, cmap, DW), perdir(d, cmap, DW), perdir(d, cmap, DW),
                perdir(d, cmap, DCOLS), shared(cmap, LANES)]

    args = (r5(u), r5(w), r5(kt), r5(qd), r5(qkd), gs.reshape(bsz, nb, ROWS, LANES))
    o_shape = jax.ShapeDtypeStruct((bsz, nb, ROWS, DW), F32)
    of, ob = pl.pallas_call(
        _scan_kernel,
        grid=(bsz, nb),
        in_specs=specs(0, fwd) + specs(1, bwd),
        out_specs=[shared(fwd, DW), shared(bwd, DW)],
        out_shape=[o_shape, o_shape],
        scratch_shapes=[pltpu.VMEM((HEADS // 2, PAIR, PAIR), F32), pltpu.VMEM((HEADS // 2, PAIR, PAIR), F32)],
        compiler_params=_cparams("parallel", "arbitrary"),
        name="delta_scan",
    )(*args, *args)
    return of.reshape(-1, DW), ob.reshape(-1, DW)


def _fourier_kernel(ct_ref, st_ref, uc_ref, us_ref, o_ref):
    o_ref[...] = (_dot(ct_ref[...], uc_ref[...]) - _dot(st_ref[...], us_ref[...])).astype(BF16)


def _fourier(p3, ct, st):
    bsz, tt, _ = p3.shape
    fw = 512
    return pl.pallas_call(
        _fourier_kernel,
        grid=(bsz, tt // ROWS),
        in_specs=[
            pl.BlockSpec((ROWS, tt), lambda b, i: (i, 0)),
            pl.BlockSpec((ROWS, tt), lambda b, i: (i, 0)),
            pl.BlockSpec((None, tt, fw), lambda b, i: (b, 0, COL_FC // fw)),
            pl.BlockSpec((None, tt, fw), lambda b, i: (b, 0, COL_FS // fw)),
        ],
        out_specs=pl.BlockSpec((None, ROWS, fw), lambda b, i: (b, i, 0)),
        out_shape=jax.ShapeDtypeStruct((bsz, tt, fw), BF16),
        compiler_params=_cparams("parallel", "arbitrary"),
        name="fourier_mix",
    )(ct, st, p3, p3)


def _merge_kernel(of_ref, ob_ref, z_ref, gt_ref, fm_ref, x_ref, mod_ref, on_ref, nf_ref,
                  wf_ref, wd_ref, wo_ref, wr_ref, br_ref, xo_ref, h2_ref, rt_ref, cnt_out_ref, cnt_ref):
    m = mod_ref[0]
    o = of_ref[...] + ob_ref[...]
    on = on_ref[...]
    parts = []
    for h in range(HEADS):
        hs = slice(h * HEAD_DIM, (h + 1) * HEAD_DIM)
        oh = o[:, hs]
        z = z_ref[:, hs].astype(F32)
        y = oh * lax.rsqrt(jnp.mean(oh * oh, axis=-1, keepdims=True) + EPS) * on
        parts.append((y * (z * _sigmoid(z))).astype(BF16))
    od = jnp.concatenate(parts, axis=1)
    pa = _dot(fm_ref[...], wf_ref[...])
    pb = _dot(od, wd_ref[...])
    d = pa.shape[1]
    ga = _sigmoid(gt_ref[:, :d].astype(F32))
    gb = _sigmoid(gt_ref[:, d:].astype(F32))
    y = _dot((ga * pa + gb * pb).astype(BF16), wo_ref[...])
    xn = x_ref[...] + m[2:3] * y
    xo_ref[...] = xn
    h2 = _norm_mod(xn, nf_ref[...], m[3:4], m[4:5])
    h2_ref[...] = h2

    a, b, _ = _split3(h2)
    wr = wr_ref[...]
    wa, wb, _ = _split3(wr)
    lg = _dot(a, wa) + _dot(a, wb) + _dot(b, wa) + br_ref[...]
    lane = _iota(lg.shape, 1).astype(F32)
    big = jnp.float32(1 << 20)
    ninf = jnp.float32(-jnp.inf)
    glog = jnp.where(lane < N_GROUPS, lg, ninf)
    gmax = jnp.max(glog, axis=-1, keepdims=True)
    grp = jnp.min(jnp.where(glog == gmax, lane, big), axis=-1, keepdims=True)
    p_grp = 1.0 / jnp.sum(jnp.exp(glog - gmax), axis=-1, keepdims=True)
    lo = N_GROUPS + grp * EXPERTS_PER_GROUP
    el = jnp.where((lane >= lo) & (lane < lo + EXPERTS_PER_GROUP), lg, ninf)
    v1 = jnp.max(el, axis=-1, keepdims=True)
    i1 = jnp.min(jnp.where(el == v1, lane, big), axis=-1, keepdims=True)
    el2 = jnp.where(lane == i1, ninf, el)
    v2 = jnp.max(el2, axis=-1, keepdims=True)
    i2 = jnp.min(jnp.where(el2 == v2, lane, big), axis=-1, keepdims=True)
    e2 = jnp.exp(v2 - v1)
    w1 = p_grp / (1.0 + e2)
    w2 = p_grp * e2 / (1.0 + e2)
    @pl.when(pl.program_id(0) == 0)
    def _():
        cnt_ref[...] = jnp.zeros_like(cnt_ref)

    rows = lg.shape[0]
    onehot = jnp.where((lane == i1) | (lane == i2), 1.0, 0.0).astype(BF16)
    earlier = (_iota((rows, rows), 1) < _iota((rows, rows), 0)).astype(BF16)
    before = _dot(earlier, onehot) + cnt_ref[0:1, :]
    rank1 = jnp.sum(jnp.where(lane == i1, before, 0.0), axis=-1, keepdims=True)
    rank2 = jnp.sum(jnp.where(lane == i2, before, 0.0), axis=-1, keepdims=True)
    cnt_ref[...] = cnt_ref[...] + _dot(jnp.ones((8, rows), BF16), onehot)
    cnt_out_ref[...] = cnt_ref[...]
    rt = jnp.where(lane == 0, i1 - N_GROUPS,
                   jnp.where(lane == 1, i2 - N_GROUPS,
                             jnp.where(lane == 2, w1,
                                       jnp.where(lane == 3, w2,
                                                 jnp.where(lane == 4, rank1, jnp.where(lane == 5, rank2, 0.0))))))
    rt_ref[...] = rt


def _merge(of, ob, p_main, fm, x2, mod, on_t, nf, wf, wd, wo, wr, br, tpb, bsz):
    n, d = x2.shape
    tile = lambda w, c=0: pl.BlockSpec((ROWS, w), lambda t: (t, c))
    full = lambda a: pl.BlockSpec(a.shape, lambda t: (0,) * a.ndim)
    return pl.pallas_call(
        _merge_kernel,
        grid=(n // ROWS,),
        in_specs=[
            tile(DW), tile(DW),
            tile(DW, COL_Z // DW),
            tile(2 * d, COL_G // (2 * d)),
            tile(fm.shape[1]),
            tile(d),
            pl.BlockSpec((1, N_MOD, d), _mod_row(tpb, bsz)),
            full(on_t), full(nf), full(wf), full(wd), full(wo), full(wr), full(br),
        ],
        out_specs=[tile(d), tile(d), tile(LANES), pl.BlockSpec((8, LANES), lambda t: (0, 0))],
        out_shape=[jax.ShapeDtypeStruct((n, d), F32), jax.ShapeDtypeStruct((n, d), F32),
                   jax.ShapeDtypeStruct((n, LANES), F32), jax.ShapeDtypeStruct((8, LANES), F32)],
        scratch_shapes=[pltpu.VMEM((8, LANES), F32)],
        compiler_params=_cparams("arbitrary"),
        name="merge_route",
    )(of, ob, p_main, p_main, fm, x2, mod, on_t, nf, wf, wd, wo, wr, br)


def _dispatch_kernel(slot_ref, h_ref, xs_in_ref, xs_ref, sem):
    del xs_in_ref

    def copy(r, k):
        return pltpu.make_async_copy(h_ref.at[pl.ds(r, 1)], xs_ref.at[pl.ds(slot_ref[k, r], 1)], sem)

    def start(r, carry):
        for k in range(TOP_K):
            copy(r, k).start()
        return carry

    def wait(r, carry):
        for k in range(TOP_K):
            copy(r, k).wait()
        return carry

    lax.fori_loop(0, ROWS, start, 0)
    lax.fori_loop(0, ROWS, wait, 0)


def _dispatch(slots, h2, n_slots):
    n, d = h2.shape
    return pl.pallas_call(
        _dispatch_kernel,
        grid=(n // ROWS,),
        in_specs=[
            pl.BlockSpec((None, TOP_K, ROWS), lambda t: (t, 0, 0), memory_space=pltpu.SMEM),
            pl.BlockSpec((ROWS, d), lambda t: (t, 0)),
            pl.BlockSpec(memory_space=pl.ANY),
        ],
        out_specs=pl.BlockSpec(memory_space=pl.ANY),
        out_shape=jax.ShapeDtypeStruct((n_slots, d), F32),
        scratch_shapes=[pltpu.SemaphoreType.DMA(())],
        input_output_aliases={2: 0},
        compiler_params=_cparams("arbitrary"),
        name="moe_dispatch",
    )(slots, h2, jnp.zeros((n_slots, d), F32))


def _expert_kernel(be_ref, nu_ref, x_ref, wg_ref, wu_ref, wd_ref, y_ref, wgb_ref, wub_ref, wdb_ref):
    i = pl.program_id(0)

    @pl.when((i == 0) | (be_ref[i] != be_ref[jnp.maximum(i - 1, 0)]))
    def _():
        wgb_ref[...] = wg_ref[...].astype(BF16)
        wub_ref[...] = wu_ref[...].astype(BF16)
        wdb_ref[...] = wd_ref[...].astype(BF16)

    @pl.when(i < nu_ref[0])
    def _():
        x = x_ref[...].astype(BF16)
        a = _dot(x, wgb_ref[...])
        u = _dot(x, wub_ref[...])
        hmid = (a * _sigmoid(a) * u).astype(BF16)
        y_ref[...] = _dot(hmid, wdb_ref[...])

    @pl.when(i >= nu_ref[0])
    def _():
        y_ref[...] = jnp.zeros_like(y_ref)


def _experts(blk_e, n_used, xs, wg, wu, wd):
    n_slots, d = xs.shape
    de = wg.shape[-1]
    grid_spec = pltpu.PrefetchScalarGridSpec(
        num_scalar_prefetch=2,
        grid=(n_slots // MOE_BLOCK,),
        in_specs=[
            pl.BlockSpec((MOE_BLOCK, d), lambda i, be, nu: (i, 0)),
            pl.BlockSpec((None, d, de), lambda i, be, nu: (be[i], 0, 0)),
            pl.BlockSpec((None, d, de), lambda i, be, nu: (be[i], 0, 0)),
            pl.BlockSpec((None, de, d), lambda i, be, nu: (be[i], 0, 0)),
        ],
        out_specs=pl.BlockSpec((MOE_BLOCK, d), lambda i, be, nu: (i, 0)),
        scratch_shapes=[pltpu.VMEM((d, de), BF16), pltpu.VMEM((d, de), BF16), pltpu.VMEM((de, d), BF16)],
    )
    return pl.pallas_call(
        _expert_kernel,
        grid_spec=grid_spec,
        out_shape=jax.ShapeDtypeStruct((n_slots, d), F32),
        compiler_params=_cparams("arbitrary"),
        name="moe_experts",
    )(blk_e, n_used, xs, wg, wu, wd)


def _combine_kernel(x_ref, y0_ref, y1_ref, rt_ref, mod_ref, o_ref):
    m = mod_ref[0]
    rt = rt_ref[...]
    y = rt[:, 2:3] * y0_ref[...] + rt[:, 3:4] * y1_ref[...]
    o_ref[...] = x_ref[...] + m[5:6] * y


def _combine(x2, y0, y1, rt, mod, tpb, bsz):
    n, d = x2.shape
    tile = lambda w: pl.BlockSpec((ROWS, w), lambda t: (t, 0))
    return pl.pallas_call(
        _combine_kernel,
        grid=(n // ROWS,),
        in_specs=[tile(d), tile(d), tile(d), tile(LANES), pl.BlockSpec((1, N_MOD, d), _mod_row(tpb, bsz))],
        out_specs=tile(d),
        out_shape=jax.ShapeDtypeStruct((n, d), F32),
        compiler_params=_cparams("parallel"),
        name="moe_combine",
    )(x2, y0, y1, rt, mod)


def _route_tables(rt, counts_f, n):
    counts = counts_f[0, N_GROUPS:N_GROUPS + N_EXPERTS].astype(jnp.int32)
    padded = (counts + MOE_BLOCK - 1) // MOE_BLOCK * MOE_BLOCK
    pad_end = jnp.cumsum(padded)
    pad_start = pad_end - padded
    eid = rt[:, 0:TOP_K].astype(jnp.int32)
    rank = rt[:, 4:4 + TOP_K].astype(jnp.int32)
    onehot = eid[:, :, None] == jnp.arange(N_EXPERTS, dtype=jnp.int32)
    slot_of = jnp.sum(jnp.where(onehot, pad_start, 0), axis=-1) + rank
    n_slots = -(-n * TOP_K // MOE_BLOCK) * MOE_BLOCK + N_EXPERTS * MOE_BLOCK
    blk_start = jnp.arange(n_slots // MOE_BLOCK, dtype=jnp.int32) * MOE_BLOCK
    blk_e = jnp.minimum(jnp.sum(blk_start[:, None] >= pad_end[None, :], axis=-1), N_EXPERTS - 1).astype(jnp.int32)
    n_used = (pad_end[-1] // MOE_BLOCK).astype(jnp.int32).reshape(1)
    return slot_of, blk_e, n_used, n_slots


def _final_kernel(x_ref, w_ref, o_ref):
    x = x_ref[...]
    o_ref[...] = x * lax.rsqrt(jnp.mean(x * x, axis=-1, keepdims=True) + EPS) * w_ref[...]


def _final_norm(x3, w, seq):
    bsz, _, d = x3.shape
    return pl.pallas_call(
        _final_kernel,
        grid=(bsz, seq // ROWS),
        in_specs=[pl.BlockSpec((None, ROWS, d), lambda b, i: (b, i, 0)), pl.BlockSpec((1, d), lambda b, i: (0, 0))],
        out_specs=pl.BlockSpec((None, ROWS, d), lambda b, i: (b, i, 0)),
        out_shape=jax.ShapeDtypeStruct((bsz, seq, d), F32),
        compiler_params=_cparams("parallel", "parallel"),
        name="final_norm",
    )(x3, w)


def _dft_tables(seq, ctx_len, group_dim):
    tt = seq + ctx_len
    r = jnp.arange(tt, dtype=jnp.int32)[:, None]
    c = jnp.arange(tt, dtype=jnp.int32)[None, :]
    lat = (r < seq) & (c < seq)
    ctx = (r >= seq) & (c >= seq)
    k_lat = (r * c) % seq
    k_ctx = ((r - seq) * (c - seq)) % ctx_len
    ang = jnp.where(lat, k_lat.astype(F32) * (2.0 * jnp.pi / seq), k_ctx.astype(F32) * (2.0 * jnp.pi / ctx_len))
    scale = jnp.where(lat, (seq * group_dim) ** -0.5, jnp.where(ctx, (ctx_len * group_dim) ** -0.5, 0.0))
    return (jnp.cos(ang) * scale).astype(BF16), (jnp.sin(ang) * scale).astype(BF16)


def _channel_dft(group_dim):
    k = (jnp.arange(group_dim, dtype=jnp.int32)[:, None] * jnp.arange(group_dim, dtype=jnp.int32)[None, :]) % group_dim
    ang = k.astype(F32) * (2.0 * jnp.pi / group_dim)
    eye = jnp.eye(FOURIER_GROUPS, dtype=F32)
    return jnp.kron(eye, jnp.cos(ang)), jnp.kron(eye, jnp.sin(ang))


def kernel(x, c, ctx, c_ctx, w_mod, b_mod, norm_mix, norm_ffn, w_in, conv_w, a_log, dt_bias, out_norm,
           w_fourier, w_delta, w_out, w_route_group, b_route_group, w_route_expert, b_route_expert,
           w_gate, w_up, w_down, final_norm):
    bsz, seq, d = x.shape
    ctx_len = ctx.shape[1]
    depth = w_mod.shape[0]
    assert d == HEADS * HEAD_DIM and ctx_len == ROWS and seq % ROWS == 0 and bsz < 16
    tt = seq + ctx_len
    tpb = tt // ROWS
    n = bsz * tt
    fw = w_fourier.shape[1]
    group_dim = fw // FOURIER_GROUPS

    cvec = jnp.zeros((16, d), F32).at[:bsz].set(c).at[bsz].set(c_ctx)
    mod_all = _modulation(cvec, w_mod, b_mod).reshape(depth, 16, N_MOD, d)
    ct, st = _dft_tables(seq, ctx_len, group_dim)
    cc, sc = _channel_dft(group_dim)

    o_a = 3 * DW
    o_b = o_a + 2 * HEADS
    o_z = o_b + 2 * HEADS
    o_f = o_z + DW
    o_g = o_f + fw

    xcur = jnp.concatenate([x, ctx], axis=1).reshape(n, d)
    for i in range(depth):
        wi = w_in[i]
        wfold_c = jnp.dot(wi[:, o_f:o_g], cc, precision=lax.Precision.HIGHEST)
        wfold_s = jnp.dot(wi[:, o_f:o_g], sc, precision=lax.Precision.HIGHEST)
        w_main = jnp.concatenate([wi[:, :o_a], wi[:, o_z:o_f], wi[:, o_g:], wfold_c, wfold_s], axis=1).astype(BF16)
        w_ab = jnp.concatenate([wi[:, o_a:o_z], jnp.zeros((d, LANES - 4 * HEADS), F32)], axis=1).astype(BF16)
        gate_params = jnp.zeros((8, LANES), F32)
        gate_params = gate_params.at[0, :2 * HEADS].set(a_log[i].reshape(-1)).at[1, :2 * HEADS].set(dt_bias[i].reshape(-1))
        mod = mod_all[i]

        p_main, ab = _inproj(xcur, mod, norm_mix[i].reshape(1, d), w_main, w_ab, tpb, bsz)
        u, w, kt, qd, qkd, gs = _prep(p_main, ab, conv_w[i], gate_params, tpb)
        of, ob = _scan(u, w, kt, qd, qkd, gs, bsz, seq // ROWS, ctx_len // ROWS)
        fm = _fourier(p_main.reshape(bsz, tt, MAIN_COLS), ct, st).reshape(n, fw)

        wr = jnp.zeros((d, LANES), F32).at[:, :N_GROUPS].set(w_route_group[i])
        wr = wr.at[:, N_GROUPS:N_GROUPS + N_EXPERTS].set(w_route_expert[i])
        br = jnp.zeros((1, LANES), F32).at[0, :N_GROUPS].set(b_route_group[i])
        br = br.at[0, N_GROUPS:N_GROUPS + N_EXPERTS].set(b_route_expert[i])
        xcur, h2, rt, counts = _merge(of, ob, p_main, fm, xcur, mod, out_norm[i].reshape(1, HEAD_DIM),
                                      norm_ffn[i].reshape(1, d), w_fourier[i].astype(BF16),
                                      w_delta[i].astype(BF16), w_out[i].astype(BF16), wr, br, tpb, bsz)

        slot_of, blk_e, n_used, n_slots = _route_tables(rt, counts, n)
        slots = jnp.swapaxes(slot_of.reshape(n // ROWS, ROWS, TOP_K), 1, 2)
        xs = _dispatch(slots, h2, n_slots)
        ys = _experts(blk_e, n_used, xs, w_gate[i], w_up[i], w_down[i])
        xcur = _combine(xcur, ys[slot_of[:, 0]], ys[slot_of[:, 1]], rt, mod, tpb, bsz)

    return _final_norm(xcur.reshape(bsz, tt, d), final_norm.reshape(1, d), seq)
```

```python
import functools

import jax
import jax.numpy as jnp
from jax import lax
from jax.experimental import pallas as pl
from jax.experimental.pallas import tpu as pltpu

F32 = jnp.float32
BF16 = jnp.bfloat16

EPS = 1e-6
ROWS = 256
CHUNK = 64
HEADS = 8
HEAD_DIM = 128
PAIR = 2 * HEAD_DIM
N_MOD = 6
N_GROUPS = 4
EXPERTS_PER_GROUP = 8
N_EXPERTS = N_GROUPS * EXPERTS_PER_GROUP
TOP_K = 2
MOE_BLOCK = 256
FOURIER_GROUPS = 4
LANES = 128
VMEM_LIMIT = 56 * 1024 * 1024


def _cparams(*sem):
    return pltpu.CompilerParams(dimension_semantics=sem, vmem_limit_bytes=VMEM_LIMIT)


def _sigmoid(x):
    return 1.0 / (1.0 + jnp.exp(-x))


def _dot(a, b):
    return jnp.dot(a, b, preferred_element_type=F32)


def _dot_nt(a, b):
    return lax.dot_general(a, b, (((1,), (1,)), ((), ())), preferred_element_type=F32)


def _dot_tn(a, b):
    return lax.dot_general(a, b, (((0,), (0,)), ((), ())), preferred_element_type=F32)


def _split3(x):
    h1 = x.astype(BF16)
    r1 = x - h1.astype(F32)
    h2 = r1.astype(BF16)
    r2 = r1 - h2.astype(F32)
    return h1, h2, r2.astype(BF16)


def _dot_x01(x, m01):
    a, b, c = _split3(x)
    return _dot(a, m01) + _dot(b, m01) + _dot(c, m01)


def _dot_01x(m01, x):
    a, b, c = _split3(x)
    return _dot(m01, a) + _dot(m01, b) + _dot(m01, c)


def _iota(shape, dim):
    return lax.broadcasted_iota(jnp.int32, shape, dim)


def _mod_kernel(c_ref, w_ref, b_ref, o_ref):
    c = c_ref[...]
    s = c * _sigmoid(c)
    w = w_ref[0]
    a, b, _ = _split3(s)
    wa, wb, _ = _split3(w)
    o_ref[0] = _dot(a, wa) + _dot(a, wb) + _dot(b, wa) + b_ref[0]


def _modulation(cvec, w_mod, b_mod):
    depth, d, n = w_mod.shape
    tn = 1536
    return pl.pallas_call(
        _mod_kernel,
        grid=(depth, n // tn),
        in_specs=[
            pl.BlockSpec((16, d), lambda i, j: (0, 0)),
            pl.BlockSpec((1, d, tn), lambda i, j: (i, 0, j)),
            pl.BlockSpec((1, 1, tn), lambda i, j: (i, 0, j)),
        ],
        out_specs=pl.BlockSpec((1, 16, tn), lambda i, j: (i, 0, j)),
        out_shape=jax.ShapeDtypeStruct((depth, 16, n), F32),
        compiler_params=_cparams("arbitrary", "arbitrary"),
        name="modulation",
    )(cvec, w_mod, b_mod.reshape(depth, 1, n))


COL_QKV = 0
COL_Z = 3 * HEADS * HEAD_DIM
COL_G = COL_Z + HEADS * HEAD_DIM
COL_FC = COL_G + 2 * 1024
COL_FS = COL_FC + 512
MAIN_COLS = COL_FS + 512
COL_TILE = 512


def _norm_mod(x, nw, shift, scale):
    ms = jnp.mean(x * x, axis=-1, keepdims=True)
    return (x * lax.rsqrt(ms + EPS) * nw) * (1.0 + scale) + shift


def _inproj_kernel(x_ref, mod_ref, nw_ref, w_ref, wab_ref, p_ref, ab_ref):
    m = mod_ref[0]
    h = _norm_mod(x_ref[...], nw_ref[...], m[0:1], m[1:2]).astype(BF16)
    for c in range(MAIN_COLS // COL_TILE):
        cs = slice(c * COL_TILE, (c + 1) * COL_TILE)
        p_ref[:, cs] = _dot(h, w_ref[:, cs]).astype(BF16)
    ab_ref[...] = _dot(h, wab_ref[...])


def _mod_row(tpb, bsz):
    return lambda t: (jnp.where(t % tpb == tpb - 1, bsz, t // tpb), 0, 0)


def _inproj(x2, mod, nw, w_main, w_ab, tpb, bsz):
    n, d = x2.shape
    return pl.pallas_call(
        _inproj_kernel,
        grid=(n // ROWS,),
        in_specs=[
            pl.BlockSpec((ROWS, d), lambda t: (t, 0)),
            pl.BlockSpec((1, N_MOD, d), _mod_row(tpb, bsz)),
            pl.BlockSpec((1, d), lambda t: (0, 0)),
            pl.BlockSpec((d, MAIN_COLS), lambda t: (0, 0)),
            pl.BlockSpec((d, LANES), lambda t: (0, 0)),
        ],
        out_specs=[
            pl.BlockSpec((ROWS, MAIN_COLS), lambda t: (t, 0)),
            pl.BlockSpec((ROWS, LANES), lambda t: (t, 0)),
        ],
        out_shape=[
            jax.ShapeDtypeStruct((n, MAIN_COLS), BF16),
            jax.ShapeDtypeStruct((n, LANES), F32),
        ],
        compiler_params=_cparams("parallel"),
        name="inproj",
    )(x2, mod, nw, w_main, w_ab)


DW = HEADS * HEAD_DIM
DCOLS = HEADS * CHUNK
SOLVE_CHUNKS = 4


def _blockdiag_rows(x, width):
    lane_blk = _iota(x.shape, 1) // width
    zero = jnp.zeros_like(x)
    return jnp.concatenate([jnp.where(lane_blk == u, x, zero) for u in range(x.shape[1] // width)], axis=0)


def _prep_kernel(tpb, qkv_ref, ab_ref, cw_ref, gp_ref,
                 u_ref, w_ref, kt_ref, qd_ref, qkd_ref, gs_ref,
                 kn_ref, qs_ref, kb_ref, rw_ref, ru_ref, dm_ref):
    t = pl.program_id(0)
    is_ctx = (t % tpb) == (tpb - 1)
    r = _iota((ROWS, 1), 0)
    rp = jnp.where(is_ctx, r, r % CHUNK)
    has_prev = rp != 0
    has_next = rp != jnp.where(is_ctx, ROWS - 1, CHUNK - 1)

    ab = ab_ref[...]
    gp = gp_ref[...]
    xg = ab + gp[1:2]
    softplus = jnp.maximum(xg, 0.0) + jnp.log(1.0 + jnp.exp(-jnp.abs(xg)))
    g = -jnp.exp(gp[0:1]) * softplus
    sig = _sigmoid(ab)

    ri = _iota((ROWS, ROWS), 0)
    ci = _iota((ROWS, ROWS), 1)
    same = (ri // CHUNK) == (ci // CHUNK)
    tri_lo = (same & (ci <= ri)).astype(BF16)
    tri_up = (same & (ci >= ri)).astype(BF16)
    ones_bd = same.astype(BF16)
    lane = _iota((ROWS, LANES), 1)
    gc = jnp.where(lane < HEADS, _dot_01x(tri_lo, g), _dot_01x(tri_up, g))
    tot = _dot_01x(ones_bd, g)
    eg = jnp.exp(gc)
    ekt = jnp.exp(tot - gc)
    gs_ref[...] = jnp.exp(tot)

    er = _iota((LANES, DCOLS), 0)
    ec = _iota((LANES, DCOLS), 1)
    di = _iota((ROWS, DCOLS), 0) % CHUNK
    dj = _iota((ROWS, DCOLS), 1) % CHUNK
    for d in range(2):
        expand = (er - d * HEADS == ec // CHUNK).astype(BF16)
        gcol = _dot_x01(gc, expand)
        grow = _dot_01x(ones_bd, jnp.where(di == dj, gcol, 0.0))
        keep = (dj <= di) if d == 0 else (dj >= di)
        dm_ref[:, d * DCOLS:(d + 1) * DCOLS] = jnp.where(keep, jnp.exp(jnp.where(keep, gcol - grow, 0.0)), 0.0)

    cw = cw_ref[...]

    def conv_silu(col):
        cs = slice(col * HEAD_DIM, (col + 1) * HEAD_DIM)
        x = qkv_ref[:, cs].astype(F32)
        prev = jnp.where(has_prev, pltpu.roll(x, 1, 0), 0.0)
        nxt = jnp.where(has_next, pltpu.roll(x, ROWS - 1, 0), 0.0)
        y = cw[1:2, cs] * x + cw[0:1, cs] * prev + cw[2:3, cs] * nxt
        return y * _sigmoid(y)

    for h in range(HEADS):
        hs = slice(h * HEAD_DIM, (h + 1) * HEAD_DIM)
        q = conv_silu(h)
        k = conv_silu(HEADS + h)
        v = conv_silu(2 * HEADS + h)
        qn = q * lax.rsqrt(jnp.sum(q * q, axis=-1, keepdims=True) + EPS) * (HEAD_DIM ** -0.5)
        kn = k * lax.rsqrt(jnp.sum(k * k, axis=-1, keepdims=True) + EPS)
        kn_ref[:, hs] = kn.astype(BF16)
        qs_ref[:, hs] = qn.astype(BF16)
        for d in range(2):
            c = d * HEADS + h
            beta = sig[:, 2 * HEADS + c:2 * HEADS + c + 1]
            egc = eg[:, c:c + 1]
            kb = kn * beta
            kb_ref[d, :, hs] = kb.astype(BF16)
            rw_ref[d, :, hs] = (kb * egc).astype(BF16)
            ru_ref[d, :, hs] = (v * beta).astype(BF16)
            kt_ref[d, :, hs] = (kn * ekt[:, c:c + 1]).astype(BF16)
            qd_ref[d, :, hs] = (qn * egc).astype(BF16)

    i64 = _iota((CHUNK, PAIR), 0)
    j64 = _iota((CHUNK, PAIR), 1) % CHUNK
    eye4 = (i64 == j64).astype(F32)
    diag2 = _iota((CHUNK, 2 * CHUNK), 0) == _iota((CHUNK, 2 * CHUNK), 1) % CHUNK
    BASE = 4
    blk = {}
    s = BASE
    while s <= CHUNK:
        blk[s] = (i64 // s) == (j64 // s)
        s *= 2

    def solve_chunks(it, carry):
        chains = []
        for cj in range(SOLVE_CHUNKS):
            rows = pl.ds(pl.multiple_of((it * SOLVE_CHUNKS + cj) * CHUNK, CHUNK), CHUNK)
            chains += [(rows, d, g) for d in range(2) for g in range(HEADS // 4)]

        def bd(x):
            return _blockdiag_rows(x.astype(BF16), CHUNK)

        ms = []
        for rows, d, g in chains:
            neg_l = []
            for p in range(2):
                pp = 2 * g + p
                ls = slice(pp * PAIR, (pp + 1) * PAIR)
                lhs = jnp.concatenate([kb_ref[d, rows, ls], qs_ref[rows, ls]], axis=0)
                gram = _dot_nt(lhs, _blockdiag_rows(kn_ref[rows, ls], HEAD_DIM))
                dm = dm_ref[rows, d * DCOLS + pp * 2 * CHUNK:d * DCOLS + (pp + 1) * 2 * CHUNK]
                neg_l.append(jnp.where(diag2, 0.0, -gram[:CHUNK] * dm))
                qkd_ref[d, rows, pp * 2 * CHUNK:(pp + 1) * 2 * CHUNK] = (gram[CHUNK:] * dm).astype(BF16)
            ms.append(jnp.concatenate(neg_l, axis=1))
        m4 = [jnp.where(blk[BASE], m, 0.0) for m in ms]
        m4sq = [_dot(x.astype(BF16), bd(x)) for x in m4]
        t_inv = [(eye4 + x) + _dot((eye4 + x).astype(BF16), bd(y)) for x, y in zip(m4, m4sq)]
        s = BASE
        while s < CHUNK:
            offd = blk[2 * s] & ~blk[s]
            ct = [_dot(jnp.where(offd, m, 0.0).astype(BF16), bd(t)) for m, t in zip(ms, t_inv)]
            t_inv = [t + _dot(t.astype(BF16), bd(c)) for t, c in zip(t_inv, ct)]
            s *= 2
        uws = []
        for (rows, d, g), t in zip(chains, t_inv):
            rhs = jnp.concatenate(
                [jnp.concatenate([ru_ref[d, rows, (4 * g + u) * HEAD_DIM:(4 * g + u + 1) * HEAD_DIM],
                                  rw_ref[d, rows, (4 * g + u) * HEAD_DIM:(4 * g + u + 1) * HEAD_DIM]], axis=1)
                 for u in range(4)], axis=0)
            uws.append(_dot(bd(t), rhs))
        for (rows, d, g), uw in zip(chains, uws):
            for u in range(4):
                hs = slice((4 * g + u) * HEAD_DIM, (4 * g + u + 1) * HEAD_DIM)
                u_ref[d, rows, hs] = uw[u * CHUNK:(u + 1) * CHUNK, :HEAD_DIM].astype(BF16)
                w_ref[d, rows, hs] = uw[u * CHUNK:(u + 1) * CHUNK, HEAD_DIM:].astype(BF16)
        return carry

    lax.fori_loop(0, ROWS // (CHUNK * SOLVE_CHUNKS), solve_chunks, 0)


def _prep(p_main, ab, conv_w, gate_params, tpb):
    n = p_main.shape[0]
    tile = lambda w: pl.BlockSpec((ROWS, w), lambda t: (t, 0))
    tile2 = lambda w: pl.BlockSpec((2, ROWS, w), lambda t: (0, t, 0))
    perdir = jax.ShapeDtypeStruct((2, n, DW), BF16)
    return pl.pallas_call(
        functools.partial(_prep_kernel, tpb),
        grid=(n // ROWS,),
        in_specs=[
            tile(3 * DW),
            tile(LANES),
            pl.BlockSpec((3, 3 * DW), lambda t: (0, 0)),
            pl.BlockSpec((8, LANES), lambda t: (0, 0)),
        ],
        out_specs=[tile2(DW), tile2(DW), tile2(DW), tile2(DW), tile2(DCOLS), tile(LANES)],
        out_shape=[perdir, perdir, perdir, perdir, jax.ShapeDtypeStruct((2, n, DCOLS), BF16),
                   jax.ShapeDtypeStruct((n, LANES), F32)],
        scratch_shapes=[pltpu.VMEM((ROWS, DW), BF16), pltpu.VMEM((ROWS, DW), BF16),
                        pltpu.VMEM((2, ROWS, DW), BF16), pltpu.VMEM((2, ROWS, DW), BF16),
                        pltpu.VMEM((2, ROWS, DW), BF16), pltpu.VMEM((ROWS, 2 * DCOLS), F32)],
        compiler_params=_cparams("parallel"),
        name="delta_prep",
    )(p_main, ab, conv_w, gate_params)


def _scan_kernel(uf, wf, ktf, qdf, qkf, gsf, ub, wb, ktb, qdb, qkb, gsb, of_ref, ob_ref, sf_ref, sb_ref):
    @pl.when(pl.program_id(1) == 0)
    def _():
        sf_ref[...] = jnp.zeros_like(sf_ref)
        sb_ref[...] = jnp.zeros_like(sb_ref)

    dirs = ((uf, wf, ktf, qdf, qkf, gsf, of_ref, sf_ref), (ub, wb, ktb, qdb, qkb, gsb, ob_ref, sb_ref))
    lane_p = _iota((8, PAIR), 1)
    bd_mask = (_iota((PAIR, PAIR), 0) // HEAD_DIM) == (_iota((PAIR, PAIR), 1) // HEAD_DIM)
    n_chunks = ROWS // CHUNK
    chains = [(d, pp) for d in range(2) for pp in range(HEADS // 2)]
    for step in range(n_chunks):
        def rows_of(d):
            ck = step if d == 0 else n_chunks - 1 - step
            return slice(ck * CHUNK, (ck + 1) * CHUNK)

        s_old, wq_s, v_new_b = [], [], []
        for d, pp in chains:
            u_ref, w_ref, kt_ref, qd_ref, qkd_ref, gs_ref, o_ref, s_ref = dirs[d]
            rows, ls = rows_of(d), slice(pp * PAIR, (pp + 1) * PAIR)
            s_old.append(s_ref[pp])
            wq_s.append(_dot(jnp.concatenate([w_ref[rows, ls], qd_ref[rows, ls]], axis=0), s_old[-1].astype(BF16)))
        for i, (d, pp) in enumerate(chains):
            u_ref = dirs[d][0]
            rows, ls = rows_of(d), slice(pp * PAIR, (pp + 1) * PAIR)
            v_new_b.append((u_ref[rows, ls].astype(F32) - wq_s[i][:CHUNK]).astype(BF16))
        for i, (d, pp) in enumerate(chains):
            u_ref, w_ref, kt_ref, qd_ref, qkd_ref, gs_ref, o_ref, s_ref = dirs[d]
            rows, ls = rows_of(d), slice(pp * PAIR, (pp + 1) * PAIR)
            qk = qkd_ref[rows, pp * 2 * CHUNK:(pp + 1) * 2 * CHUNK]
            o_ref[rows, ls] = (wq_s[i][CHUNK:] + _dot(qk, _blockdiag_rows(v_new_b[i], HEAD_DIM))).astype(BF16)
        for i, (d, pp) in enumerate(chains):
            u_ref, w_ref, kt_ref, qd_ref, qkd_ref, gs_ref, o_ref, s_ref = dirs[d]
            rows, ls = rows_of(d), slice(pp * PAIR, (pp + 1) * PAIR)
            gs = gs_ref[rows.start:rows.start + 8, :]
            c0 = d * HEADS + 2 * pp
            decay = jnp.where(lane_p < HEAD_DIM, gs[:, c0:c0 + 1], gs[:, c0 + 1:c0 + 2])
            upd = _dot_tn(kt_ref[rows, ls], v_new_b[i])
            s_dec = (s_old[i].reshape(PAIR // 8, 8, PAIR) * decay[None]).reshape(PAIR, PAIR)
            s_ref[pp] = s_dec + jnp.where(bd_mask, upd, 0.0)


def _scan(u, w, kt, qd, qkd, gs, bsz, n_lat_blocks, n_ctx_blocks):
    nb = n_lat_blocks + n_ctx_blocks
    fwd = lambda j: jnp.where(j < n_ctx_blocks, n_lat_blocks + j, j - n_ctx_blocks)
    bwd = lambda j: nb - 1 - j
    r5 = lambda a: a.reshape(2, bsz, nb, ROWS, a.shape[-1])

    def perdir(d, cmap, w):
        return pl.BlockSpec((None, None, None, ROWS, w), lambda b, j: (d, b, cmap(j), 0, 0))

    def shared(cmap, w):
        return pl.BlockSpec((None, None, ROWS, w), lambda b, j: (b, cmap(j), 0, 0))

    def specs(d, cmap):
        return [perdir(d, cmap, DW), perdir(d, cmap, DW), perdir(d, cmap, DW), perdir(d, cmap, DW),
                perdir(d, cmap, DCOLS), shared(cmap, LANES)]

    args = (r5(u), r5(w), r5(kt), r5(qd), r5(qkd), gs.reshape(bsz, nb, ROWS, LANES))
    o_shape = jax.ShapeDtypeStruct((bsz, nb, ROWS, DW), BF16)
    of, ob = pl.pallas_call(
        _scan_kernel,
        grid=(bsz, nb),
        in_specs=specs(0, fwd) + specs(1, bwd),
        out_specs=[shared(fwd, DW), shared(bwd, DW)],
        out_shape=[o_shape, o_shape],
        scratch_shapes=[pltpu.VMEM((HEADS // 2, PAIR, PAIR), F32), pltpu.VMEM((HEADS // 2, PAIR, PAIR), F32)],
        compiler_params=_cparams("parallel", "arbitrary"),
        name="delta_scan",
    )(*args, *args)
    return of.reshape(-1, DW), ob.reshape(-1, DW)


FFT_GROUP = 8
FCOLS = 1024


def _fft_stage1_kernel(*refs):
    x_refs = refs[:FFT_GROUP]
    f_ref, twc_ref, tws_ref, o_ref = refs[FFT_GROUP:]
    r = f_ref.shape[1]
    half = FCOLS // 2
    f = f_ref[...]
    for j in range(FFT_GROUP):
        rr = _dot(f, x_refs[j][0:r, :])
        a_r = rr[:r, :half] - rr[r:, half:]
        a_i = -(rr[:r, half:] + rr[r:, :half])
        c = jnp.concatenate([twc_ref[j]] * (half // LANES), axis=1)
        sn = jnp.concatenate([tws_ref[j]] * (half // LANES), axis=1)
        o_ref[j * r:(j + 1) * r, :half] = (a_r * c + a_i * sn).astype(BF16)
        o_ref[j * r:(j + 1) * r, half:] = (a_i * c - a_r * sn).astype(BF16)


def _fft_stage2_kernel(x_ref, gc_ref, gs_ref, o_ref):
    half = FCOLS // 2
    gc = gc_ref[...]
    gs = gs_ref[...]
    for j in range(x_ref.shape[1] // FCOLS):
        b_r = x_ref[:, j * FCOLS:j * FCOLS + half]
        b_i = x_ref[:, j * FCOLS + half:(j + 1) * FCOLS]
        o_ref[:, j * half:(j + 1) * half] = (_dot(gc, b_r) + _dot(gs, b_i)).astype(BF16)


def _fourier_ctx_kernel(ct_ref, st_ref, x_ref, o_ref):
    half = FCOLS // 2
    o_ref[...] = (_dot(ct_ref[...], x_ref[:, :half]) - _dot(st_ref[...], x_ref[:, half:])).astype(BF16)


def _fourier(p_main, bsz, seq, ctx_len, group_dim):
    tt = seq + ctx_len
    r = seq // CHUNK
    half = FCOLS // 2
    cols_per_tok = MAIN_COLS // FCOLS
    fcol = COL_FC // FCOLS
    grp2 = min(FFT_GROUP, r)

    def angles(n_rows, n_cols, period):
        k = (jnp.arange(n_rows, dtype=jnp.int32)[:, None] * jnp.arange(n_cols, dtype=jnp.int32)[None, :]) % period
        return k.astype(F32) * (2.0 * jnp.pi / period)

    a1 = angles(r, r, r)
    f1 = jnp.concatenate([jnp.cos(a1), jnp.sin(a1)], axis=0).astype(BF16)
    atw = angles(CHUNK, r, seq)
    twc = jnp.broadcast_to(jnp.cos(atw)[:, :, None], (CHUNK, r, LANES))
    tws = jnp.broadcast_to(jnp.sin(atw)[:, :, None], (CHUNK, r, LANES))
    a2 = angles(CHUNK, CHUNK, CHUNK)
    scale = (seq * group_dim) ** -0.5
    gc = (jnp.cos(a2) * scale).astype(BF16)
    gs = (jnp.sin(a2) * scale).astype(BF16)

    pv = p_main.reshape(bsz, tt // CHUNK, CHUNK * MAIN_COLS)
    x_specs = [pl.BlockSpec((None, tt // CHUNK, FCOLS),
                            lambda b, g, j=j: (b, 0, (g * FFT_GROUP + j) * cols_per_tok + fcol))
               for j in range(FFT_GROUP)]
    stage1 = pl.pallas_call(
        _fft_stage1_kernel,
        grid=(bsz, CHUNK // FFT_GROUP),
        in_specs=x_specs + [
            pl.BlockSpec((2 * r, r), lambda b, g: (0, 0)),
            pl.BlockSpec((FFT_GROUP, r, LANES), lambda b, g: (g, 0, 0)),
            pl.BlockSpec((FFT_GROUP, r, LANES), lambda b, g: (g, 0, 0)),
        ],
        out_specs=pl.BlockSpec((None, FFT_GROUP * r, FCOLS), lambda b, g: (b, g, 0)),
        out_shape=jax.ShapeDtypeStruct((bsz, CHUNK * r, FCOLS), BF16),
        compiler_params=_cparams("parallel", "arbitrary"),
        name="fourier_stage1",
    )(*([pv] * FFT_GROUP), f1, twc, tws)
    lat = pl.pallas_call(
        _fft_stage2_kernel,
        grid=(bsz, r // grp2),
        in_specs=[
            pl.BlockSpec((None, CHUNK, grp2 * FCOLS), lambda b, g: (b, 0, g)),
            pl.BlockSpec((CHUNK, CHUNK), lambda b, g: (0, 0)),
            pl.BlockSpec((CHUNK, CHUNK), lambda b, g: (0, 0)),
        ],
        out_specs=pl.BlockSpec((None, CHUNK, grp2 * half), lambda b, g: (b, 0, g)),
        out_shape=jax.ShapeDtypeStruct((bsz, CHUNK, r * half), BF16),
        compiler_params=_cparams("parallel", "arbitrary"),
        name="fourier_stage2",
    )(stage1.reshape(bsz, CHUNK, r * FCOLS), gc, gs)

    ac = angles(ctx_len, ctx_len, ctx_len)
    cscale = (ctx_len * group_dim) ** -0.5
    ctx_out = pl.pallas_call(
        _fourier_ctx_kernel,
        grid=(bsz,),
        in_specs=[
            pl.BlockSpec((ctx_len, ctx_len), lambda b: (0, 0)),
            pl.BlockSpec((ctx_len, ctx_len), lambda b: (0, 0)),
            pl.BlockSpec((None, ctx_len, FCOLS), lambda b: (b, seq // ctx_len, fcol)),
        ],
        out_specs=pl.BlockSpec((None, ctx_len, half), lambda b: (b, 0, 0)),
        out_shape=jax.ShapeDtypeStruct((bsz, ctx_len, half), BF16),
        compiler_params=_cparams("parallel"),
        name="fourier_ctx",
    )((jnp.cos(ac) * cscale).astype(BF16), (jnp.sin(ac) * cscale).astype(BF16),
      p_main.reshape(bsz, tt, MAIN_COLS))
    return jnp.concatenate([lat.reshape(bsz, seq, half), ctx_out], axis=1).reshape(bsz * tt, half)


def _merge_kernel(of_ref, ob_ref, z_ref, gt_ref, fm_ref, x_ref, mod_ref, on_ref, nf_ref,
                  wf_ref, wd_ref, wo_ref, wrh_ref, wrl_ref, br_ref, xo_ref, h2_ref, rt_ref, cnt_out_ref, cnt_ref):
    m = mod_ref[0]
    o = of_ref[...].astype(F32) + ob_ref[...].astype(F32)
    on = on_ref[...]
    parts = []
    for h in range(HEADS):
        hs = slice(h * HEAD_DIM, (h + 1) * HEAD_DIM)
        oh = o[:, hs]
        z = z_ref[:, hs].astype(F32)
        y = oh * lax.rsqrt(jnp.mean(oh * oh, axis=-1, keepdims=True) + EPS) * on
        parts.append((y * (z * _sigmoid(z))).astype(BF16))
    od = jnp.concatenate(parts, axis=1)
    pa = _dot(fm_ref[...], wf_ref[...])
    pb = _dot(od, wd_ref[...])
    d = pa.shape[1]
    ga = _sigmoid(gt_ref[:, :d].astype(F32))
    gb = _sigmoid(gt_ref[:, d:].astype(F32))
    y = _dot((ga * pa + gb * pb).astype(BF16), wo_ref[...])
    xn = x_ref[...] + m[2:3] * y
    xo_ref[...] = xn
    h2 = _norm_mod(xn, nf_ref[...], m[3:4], m[4:5])
    h2_ref[...] = h2

    a, b, _ = _split3(h2)
    lg = _dot(a, wrh_ref[...]) + _dot(a, wrl_ref[...]) + _dot(b, wrh_ref[...]) + br_ref[...]
    lane = _iota(lg.shape, 1).astype(F32)
    big = jnp.float32(1 << 20)
    ninf = jnp.float32(-jnp.inf)
    glog = jnp.where(lane < N_GROUPS, lg, ninf)
    gmax = jnp.max(glog, axis=-1, keepdims=True)
    grp = jnp.min(jnp.where(glog == gmax, lane, big), axis=-1, keepdims=True)
    p_grp = 1.0 / jnp.sum(jnp.exp(glog - gmax), axis=-1, keepdims=True)
    lo = N_GROUPS + grp * EXPERTS_PER_GROUP
    el = jnp.where((lane >= lo) & (lane < lo + EXPERTS_PER_GROUP), lg, ninf)
    v1 = jnp.max(el, axis=-1, keepdims=True)
    i1 = jnp.min(jnp.where(el == v1, lane, big), axis=-1, keepdims=True)
    el2 = jnp.where(lane == i1, ninf, el)
    v2 = jnp.max(el2, axis=-1, keepdims=True)
    i2 = jnp.min(jnp.where(el2 == v2, lane, big), axis=-1, keepdims=True)
    e2 = jnp.exp(v2 - v1)
    w1 = p_grp / (1.0 + e2)
    w2 = p_grp * e2 / (1.0 + e2)
    @pl.when(pl.program_id(0) == 0)
    def _():
        cnt_ref[...] = jnp.zeros_like(cnt_ref)

    rows = lg.shape[0]
    onehot = jnp.where((lane == i1) | (lane == i2), 1.0, 0.0).astype(BF16)
    earlier = (_iota((rows, rows), 1) < _iota((rows, rows), 0)).astype(BF16)
    before = _dot(earlier, onehot) + cnt_ref[0:1, :]
    rank1 = jnp.sum(jnp.where(lane == i1, before, 0.0), axis=-1, keepdims=True)
    rank2 = jnp.sum(jnp.where(lane == i2, before, 0.0), axis=-1, keepdims=True)
    cnt_ref[...] = cnt_ref[...] + _dot(jnp.ones((8, rows), BF16), onehot)
    cnt_out_ref[...] = cnt_ref[...]
    rt = jnp.where(lane == 0, i1 - N_GROUPS,
                   jnp.where(lane == 1, i2 - N_GROUPS,
                             jnp.where(lane == 2, w1,
                                       jnp.where(lane == 3, w2,
                                                 jnp.where(lane == 4, rank1, jnp.where(lane == 5, rank2, 0.0))))))
    rt_ref[...] = rt


def _merge(of, ob, p_main, fm, x2, mod, on_t, nf, wf, wd, wo, wr, br, tpb, bsz):
    wr_hi = wr.astype(BF16)
    wr_lo = (wr - wr_hi.astype(F32)).astype(BF16)
    n, d = x2.shape
    tile = lambda w, c=0: pl.BlockSpec((ROWS, w), lambda t: (t, c))
    full = lambda a: pl.BlockSpec(a.shape, lambda t: (0,) * a.ndim)
    return pl.pallas_call(
        _merge_kernel,
        grid=(n // ROWS,),
        in_specs=[
            tile(DW), tile(DW),
            tile(DW, COL_Z // DW),
            tile(2 * d, COL_G // (2 * d)),
            tile(fm.shape[1]),
            tile(d),
            pl.BlockSpec((1, N_MOD, d), _mod_row(tpb, bsz)),
            full(on_t), full(nf), full(wf), full(wd), full(wo), full(wr_hi), full(wr_lo), full(br),
        ],
        out_specs=[tile(d), tile(d), tile(LANES), pl.BlockSpec((8, LANES), lambda t: (0, 0))],
        out_shape=[jax.ShapeDtypeStruct((n, d), F32), jax.ShapeDtypeStruct((n, d), F32),
                   jax.ShapeDtypeStruct((n, LANES), F32), jax.ShapeDtypeStruct((8, LANES), F32)],
        scratch_shapes=[pltpu.VMEM((8, LANES), F32)],
        compiler_params=_cparams("arbitrary"),
        name="merge_route",
    )(of, ob, p_main, p_main, fm, x2, mod, on_t, nf, wf, wd, wo, wr_hi, wr_lo, br)


def _dispatch_kernel(slot_ref, h_ref, xs_in_ref, xs_ref, sem):
    del xs_in_ref

    def copy(r, k):
        return pltpu.make_async_copy(h_ref.at[pl.ds(r, 1)], xs_ref.at[pl.ds(slot_ref[k, r], 1)], sem)

    def start(r8, carry):
        for j in range(8):
            for k in range(TOP_K):
                copy(r8 * 8 + j, k).start()
        return carry

    lax.fori_loop(0, ROWS // 8, start, 0)
    for r in range(ROWS):
        for k in range(TOP_K):
            copy(r, k).wait()


def _dispatch(slots, h2, n_slots):
    n, d = h2.shape
    return pl.pallas_call(
        _dispatch_kernel,
        grid=(n // ROWS,),
        in_specs=[
            pl.BlockSpec((None, TOP_K, ROWS), lambda t: (t, 0, 0), memory_space=pltpu.SMEM),
            pl.BlockSpec((ROWS, d), lambda t: (t, 0)),
            pl.BlockSpec(memory_space=pl.ANY),
        ],
        out_specs=pl.BlockSpec(memory_space=pl.ANY),
        out_shape=jax.ShapeDtypeStruct((n_slots, d), F32),
        scratch_shapes=[pltpu.SemaphoreType.DMA(())],
        input_output_aliases={2: 0},
        compiler_params=_cparams("arbitrary"),
        name="moe_dispatch",
    )(slots, h2, jnp.zeros((n_slots, d), F32))


def _expert_kernel(be_ref, nu_ref, x_ref, wg_ref, wu_ref, wd_ref, y_ref, wgb_ref, wub_ref, wdb_ref):
    i = pl.program_id(0)

    @pl.when((i == 0) | (be_ref[i] != be_ref[jnp.maximum(i - 1, 0)]))
    def _():
        wgb_ref[...] = wg_ref[...].astype(BF16)
        wub_ref[...] = wu_ref[...].astype(BF16)
        wdb_ref[...] = wd_ref[...].astype(BF16)

    @pl.when(i < nu_ref[0])
    def _():
        x = x_ref[...].astype(BF16)
        a = _dot(x, wgb_ref[...])
        u = _dot(x, wub_ref[...])
        hmid = (a * _sigmoid(a) * u).astype(BF16)
        y_ref[...] = _dot(hmid, wdb_ref[...])

    @pl.when(i >= nu_ref[0])
    def _():
        y_ref[...] = jnp.zeros_like(y_ref)


def _experts(blk_e, n_used, xs, wg, wu, wd, layer):
    n_slots, d = xs.shape
    de = wg.shape[-1]
    grid_spec = pltpu.PrefetchScalarGridSpec(
        num_scalar_prefetch=2,
        grid=(n_slots // MOE_BLOCK,),
        in_specs=[
            pl.BlockSpec((MOE_BLOCK, d), lambda i, be, nu: (i, 0)),
            pl.BlockSpec((None, None, d, de), lambda i, be, nu: (layer, be[i], 0, 0)),
            pl.BlockSpec((None, None, d, de), lambda i, be, nu: (layer, be[i], 0, 0)),
            pl.BlockSpec((None, None, de, d), lambda i, be, nu: (layer, be[i], 0, 0)),
        ],
        out_specs=pl.BlockSpec((MOE_BLOCK, d), lambda i, be, nu: (i, 0)),
        scratch_shapes=[pltpu.VMEM((d, de), BF16), pltpu.VMEM((d, de), BF16), pltpu.VMEM((de, d), BF16)],
    )
    return pl.pallas_call(
        _expert_kernel,
        grid_spec=grid_spec,
        out_shape=jax.ShapeDtypeStruct((n_slots, d), F32),
        compiler_params=_cparams("arbitrary"),
        name="moe_experts",
    )(blk_e, n_used, xs, wg, wu, wd)


def _combine_kernel(x_ref, y0_ref, y1_ref, rt_ref, mod_ref, o_ref):
    m = mod_ref[0]
    rt = rt_ref[...]
    y = rt[:, 2:3] * y0_ref[...] + rt[:, 3:4] * y1_ref[...]
    o_ref[...] = x_ref[...] + m[5:6] * y


def _combine(x2, y0, y1, rt, mod, tpb, bsz):
    n, d = x2.shape
    tile = lambda w: pl.BlockSpec((ROWS, w), lambda t: (t, 0))
    return pl.pallas_call(
        _combine_kernel,
        grid=(n // ROWS,),
        in_specs=[tile(d), tile(d), tile(d), tile(LANES), pl.BlockSpec((1, N_MOD, d), _mod_row(tpb, bsz))],
        out_specs=tile(d),
        out_shape=jax.ShapeDtypeStruct((n, d), F32),
        compiler_params=_cparams("parallel"),
        name="moe_combine",
    )(x2, y0, y1, rt, mod)


def _route_tables(rt, counts_f, n):
    counts = counts_f[0, N_GROUPS:N_GROUPS + N_EXPERTS].astype(jnp.int32)
    padded = (counts + MOE_BLOCK - 1) // MOE_BLOCK * MOE_BLOCK
    pad_end = jnp.cumsum(padded)
    pad_start = pad_end - padded
    eid = rt[:, 0:TOP_K].astype(jnp.int32)
    rank = rt[:, 4:4 + TOP_K].astype(jnp.int32)
    onehot = eid[:, :, None] == jnp.arange(N_EXPERTS, dtype=jnp.int32)
    slot_of = jnp.sum(jnp.where(onehot, pad_start, 0), axis=-1) + rank
    n_slots = -(-n * TOP_K // MOE_BLOCK) * MOE_BLOCK + N_EXPERTS * MOE_BLOCK
    blk_start = jnp.arange(n_slots // MOE_BLOCK, dtype=jnp.int32) * MOE_BLOCK
    blk_e = jnp.minimum(jnp.sum(blk_start[:, None] >= pad_end[None, :], axis=-1), N_EXPERTS - 1).astype(jnp.int32)
    n_used = (pad_end[-1] // MOE_BLOCK).astype(jnp.int32).reshape(1)
    return slot_of, blk_e, n_used, n_slots


def _final_kernel(x_ref, w_ref, o_ref):
    x = x_ref[...]
    o_ref[...] = x * lax.rsqrt(jnp.mean(x * x, axis=-1, keepdims=True) + EPS) * w_ref[...]


def _final_norm(x3, w, seq):
    bsz, _, d = x3.shape
    return pl.pallas_call(
        _final_kernel,
        grid=(bsz, seq // ROWS),
        in_specs=[pl.BlockSpec((None, ROWS, d), lambda b, i: (b, i, 0)), pl.BlockSpec((1, d), lambda b, i: (0, 0))],
        out_specs=pl.BlockSpec((None, ROWS, d), lambda b, i: (b, i, 0)),
        out_shape=jax.ShapeDtypeStruct((bsz, seq, d), F32),
        compiler_params=_cparams("parallel", "parallel"),
        name="final_norm",
    )(x3, w)


def _channel_dft(group_dim):
    k = (jnp.arange(group_dim, dtype=jnp.int32)[:, None] * jnp.arange(group_dim, dtype=jnp.int32)[None, :]) % group_dim
    ang = k.astype(F32) * (2.0 * jnp.pi / group_dim)
    eye = jnp.eye(FOURIER_GROUPS, dtype=F32)
    return jnp.kron(eye, jnp.cos(ang)), jnp.kron(eye, jnp.sin(ang))


def kernel(x, c, ctx, c_ctx, w_mod, b_mod, norm_mix, norm_ffn, w_in, conv_w, a_log, dt_bias, out_norm,
           w_fourier, w_delta, w_out, w_route_group, b_route_group, w_route_expert, b_route_expert,
           w_gate, w_up, w_down, final_norm):
    bsz, seq, d = x.shape
    ctx_len = ctx.shape[1]
    depth = w_mod.shape[0]
    assert d == HEADS * HEAD_DIM and ctx_len == ROWS and seq % ROWS == 0 and bsz < 16
    tt = seq + ctx_len
    tpb = tt // ROWS
    n = bsz * tt
    fw = w_fourier.shape[1]
    group_dim = fw // FOURIER_GROUPS

    cvec = jnp.zeros((16, d), F32).at[:bsz].set(c).at[bsz].set(c_ctx)
    mod_all = _modulation(cvec, w_mod, b_mod).reshape(depth, 16, N_MOD, d)
    cc, sc = _channel_dft(group_dim)

    o_a = 3 * DW
    o_b = o_a + 2 * HEADS
    o_z = o_b + 2 * HEADS
    o_f = o_z + DW
    o_g = o_f + fw

    xcur = jnp.concatenate([x, ctx], axis=1).reshape(n, d)
    for i in range(depth):
        wi = w_in[i]
        wfold_c = jnp.dot(wi[:, o_f:o_g], cc, precision=lax.Precision.HIGHEST)
        wfold_s = jnp.dot(wi[:, o_f:o_g], sc, precision=lax.Precision.HIGHEST)
        w_main = jnp.concatenate([wi[:, :o_a], wi[:, o_z:o_f], wi[:, o_g:], wfold_c, wfold_s], axis=1).astype(BF16)
        w_ab = jnp.concatenate([wi[:, o_a:o_z], jnp.zeros((d, LANES - 4 * HEADS), F32)], axis=1).astype(BF16)
        gate_params = jnp.zeros((8, LANES), F32)
        gate_params = gate_params.at[0, :2 * HEADS].set(a_log[i].reshape(-1)).at[1, :2 * HEADS].set(dt_bias[i].reshape(-1))
        mod = mod_all[i]

        p_main, ab = _inproj(xcur, mod, norm_mix[i].reshape(1, d), w_main, w_ab, tpb, bsz)
        u, w, kt, qd, qkd, gs = _prep(p_main, ab, conv_w[i], gate_params, tpb)
        of, ob = _scan(u, w, kt, qd, qkd, gs, bsz, seq // ROWS, ctx_len // ROWS)
        fm = _fourier(p_main, bsz, seq, ctx_len, group_dim)

        wr = jnp.zeros((d, LANES), F32).at[:, :N_GROUPS].set(w_route_group[i])
        wr = wr.at[:, N_GROUPS:N_GROUPS + N_EXPERTS].set(w_route_expert[i])
        br = jnp.zeros((1, LANES), F32).at[0, :N_GROUPS].set(b_route_group[i])
        br = br.at[0, N_GROUPS:N_GROUPS + N_EXPERTS].set(b_route_expert[i])
        xcur, h2, rt, counts = _merge(of, ob, p_main, fm, xcur, mod, out_norm[i].reshape(1, HEAD_DIM),
                                      norm_ffn[i].reshape(1, d), w_fourier[i].astype(BF16),
                                      w_delta[i].astype(BF16), w_out[i].astype(BF16), wr, br, tpb, bsz)

        slot_of, blk_e, n_used, n_slots = _route_tables(rt, counts, n)
        slots = jnp.swapaxes(slot_of.reshape(n // ROWS, ROWS, TOP_K), 1, 2)
        xs = _dispatch(slots, h2, n_slots)
        ys = _experts(blk_e, n_used, xs, w_gate, w_up, w_down, i)
        xcur = _combine(xcur, ys[slot_of[:, 0]], ys[slot_of[:, 1]], rt, mod, tpb, bsz)

    return _final_norm(xcur.reshape(bsz, tt, d), final_norm.reshape(1, d), seq)
```

```python
import functools

import jax
import jax.numpy as jnp
from jax import lax
from jax.experimental import pallas as pl
from jax.experimental.pallas import tpu as pltpu

F32 = jnp.float32
BF16 = jnp.bfloat16

EPS = 1e-6
ROWS = 256
CHUNK = 64
HEADS = 8
HEAD_DIM = 128
PAIR = 2 * HEAD_DIM
N_MOD = 6
N_GROUPS = 4
EXPERTS_PER_GROUP = 8
N_EXPERTS = N_GROUPS * EXPERTS_PER_GROUP
TOP_K = 2
MOE_BLOCK = 256
FOURIER_GROUPS = 4
LANES = 128
VMEM_LIMIT = 56 * 1024 * 1024


def _cparams(*sem):
    return pltpu.CompilerParams(dimension_semantics=sem, vmem_limit_bytes=VMEM_LIMIT)


def _sigmoid(x):
    return 1.0 / (1.0 + jnp.exp(-x))


def _dot(a, b):
    return jnp.dot(a, b, preferred_element_type=F32)


def _dot_nt(a, b):
    return lax.dot_general(a, b, (((1,), (1,)), ((), ())), preferred_element_type=F32)


def _dot_tn(a, b):
    return lax.dot_general(a, b, (((0,), (0,)), ((), ())), preferred_element_type=F32)


def _split3(x):
    h1 = x.astype(BF16)
    r1 = x - h1.astype(F32)
    h2 = r1.astype(BF16)
    r2 = r1 - h2.astype(F32)
    return h1, h2, r2.astype(BF16)


def _dot_x01(x, m01):
    a, b, c = _split3(x)
    return _dot(a, m01) + _dot(b, m01) + _dot(c, m01)


def _dot_01x(m01, x):
    a, b, c = _split3(x)
    return _dot(m01, a) + _dot(m01, b) + _dot(m01, c)


def _iota(shape, dim):
    return lax.broadcasted_iota(jnp.int32, shape, dim)


def _mod_kernel(c_ref, w_ref, b_ref, o_ref):
    c = c_ref[...]
    s = c * _sigmoid(c)
    w = w_ref[0]
    a, b, _ = _split3(s)
    wa, wb, _ = _split3(w)
    o_ref[0] = _dot(a, wa) + _dot(a, wb) + _dot(b, wa) + b_ref[0]


def _modulation(cvec, w_mod, b_mod):
    depth, d, n = w_mod.shape
    tn = 1536
    return pl.pallas_call(
        _mod_kernel,
        grid=(depth, n // tn),
        in_specs=[
            pl.BlockSpec((16, d), lambda i, j: (0, 0)),
            pl.BlockSpec((1, d, tn), lambda i, j: (i, 0, j)),
            pl.BlockSpec((1, 1, tn), lambda i, j: (i, 0, j)),
        ],
        out_specs=pl.BlockSpec((1, 16, tn), lambda i, j: (i, 0, j)),
        out_shape=jax.ShapeDtypeStruct((depth, 16, n), F32),
        compiler_params=_cparams("arbitrary", "arbitrary"),
        name="modulation",
    )(cvec, w_mod, b_mod.reshape(depth, 1, n))


COL_QKV = 0
COL_Z = 3 * HEADS * HEAD_DIM
COL_G = COL_Z + HEADS * HEAD_DIM
COL_FC = COL_G + 2 * 1024
COL_FS = COL_FC + 512
MAIN_COLS = COL_FS + 512
COL_TILE = 512


def _norm_mod(x, nw, shift, scale):
    ms = jnp.mean(x * x, axis=-1, keepdims=True)
    return (x * lax.rsqrt(ms + EPS) * nw) * (1.0 + scale) + shift


def _inproj_kernel(x_ref, mod_ref, nw_ref, w_ref, wab_ref, p_ref, ab_ref):
    m = mod_ref[0]
    h = _norm_mod(x_ref[...], nw_ref[...], m[0:1], m[1:2]).astype(BF16)
    for c in range(MAIN_COLS // COL_TILE):
        cs = slice(c * COL_TILE, (c + 1) * COL_TILE)
        p_ref[:, cs] = _dot(h, w_ref[:, cs]).astype(BF16)
    ab_ref[...] = _dot(h, wab_ref[...])


def _mod_row(tpb, bsz):
    return lambda t: (jnp.where(t % tpb == tpb - 1, bsz, t // tpb), 0, 0)


def _inproj(x2, mod, nw, w_main, w_ab, tpb, bsz):
    n, d = x2.shape
    return pl.pallas_call(
        _inproj_kernel,
        grid=(n // ROWS,),
        in_specs=[
            pl.BlockSpec((ROWS, d), lambda t: (t, 0)),
            pl.BlockSpec((1, N_MOD, d), _mod_row(tpb, bsz)),
            pl.BlockSpec((1, d), lambda t: (0, 0)),
            pl.BlockSpec((d, MAIN_COLS), lambda t: (0, 0)),
            pl.BlockSpec((d, LANES), lambda t: (0, 0)),
        ],
        out_specs=[
            pl.BlockSpec((ROWS, MAIN_COLS), lambda t: (t, 0)),
            pl.BlockSpec((ROWS, LANES), lambda t: (t, 0)),
        ],
        out_shape=[
            jax.ShapeDtypeStruct((n, MAIN_COLS), BF16),
            jax.ShapeDtypeStruct((n, LANES), F32),
        ],
        compiler_params=_cparams("parallel"),
        name="inproj",
    )(x2, mod, nw, w_main, w_ab)


DW = HEADS * HEAD_DIM
DCOLS = HEADS * CHUNK
SOLVE_CHUNKS = 4


def _blockdiag_rows(x, width):
    lane_blk = _iota(x.shape, 1) // width
    zero = jnp.zeros_like(x)
    return jnp.concatenate([jnp.where(lane_blk == u, x, zero) for u in range(x.shape[1] // width)], axis=0)


def _prep_kernel(tpb, qkv_ref, ab_ref, cw_ref, gp_ref,
                 u_ref, w_ref, kt_ref, qd_ref, qkd_ref, gs_ref,
                 kn_ref, qs_ref, kb_ref, rw_ref, ru_ref, dm_ref):
    t = pl.program_id(0)
    is_ctx = (t % tpb) == (tpb - 1)
    r = _iota((ROWS, 1), 0)
    rp = jnp.where(is_ctx, r, r % CHUNK)
    has_prev = rp != 0
    has_next = rp != jnp.where(is_ctx, ROWS - 1, CHUNK - 1)

    ab = ab_ref[...]
    gp = gp_ref[...]
    xg = ab + gp[1:2]
    softplus = jnp.maximum(xg, 0.0) + jnp.log(1.0 + jnp.exp(-jnp.abs(xg)))
    g = -jnp.exp(gp[0:1]) * softplus
    sig = _sigmoid(ab)

    ri = _iota((ROWS, ROWS), 0)
    ci = _iota((ROWS, ROWS), 1)
    same = (ri // CHUNK) == (ci // CHUNK)
    tri_lo = (same & (ci <= ri)).astype(BF16)
    tri_up = (same & (ci >= ri)).astype(BF16)
    ones_bd = same.astype(BF16)
    lane = _iota((ROWS, LANES), 1)
    gc = jnp.where(lane < HEADS, _dot_01x(tri_lo, g), _dot_01x(tri_up, g))
    tot = _dot_01x(ones_bd, g)
    eg = jnp.exp(gc)
    ekt = jnp.exp(tot - gc)
    gs_ref[...] = jnp.exp(tot)

    er = _iota((LANES, DCOLS), 0)
    ec = _iota((LANES, DCOLS), 1)
    di = _iota((ROWS, DCOLS), 0) % CHUNK
    dj = _iota((ROWS, DCOLS), 1) % CHUNK
    for d in range(2):
        expand = (er - d * HEADS == ec // CHUNK).astype(BF16)
        gcol = _dot_x01(gc, expand)
        grow = _dot_01x(ones_bd, jnp.where(di == dj, gcol, 0.0))
        keep = (dj <= di) if d == 0 else (dj >= di)
        dm_ref[:, d * DCOLS:(d + 1) * DCOLS] = jnp.where(keep, jnp.exp(jnp.where(keep, gcol - grow, 0.0)), 0.0)

    cw = cw_ref[...]

    def conv_silu(col):
        cs = slice(col * HEAD_DIM, (col + 1) * HEAD_DIM)
        x = qkv_ref[:, cs].astype(F32)
        prev = jnp.where(has_prev, pltpu.roll(x, 1, 0), 0.0)
        nxt = jnp.where(has_next, pltpu.roll(x, ROWS - 1, 0), 0.0)
        y = cw[1:2, cs] * x + cw[0:1, cs] * prev + cw[2:3, cs] * nxt
        return y * _sigmoid(y)

    for h in range(HEADS):
        hs = slice(h * HEAD_DIM, (h + 1) * HEAD_DIM)
        q = conv_silu(h)
        k = conv_silu(HEADS + h)
        v = conv_silu(2 * HEADS + h)
        qn = q * lax.rsqrt(jnp.sum(q * q, axis=-1, keepdims=True) + EPS) * (HEAD_DIM ** -0.5)
        kn = k * lax.rsqrt(jnp.sum(k * k, axis=-1, keepdims=True) + EPS)
        kn_ref[:, hs] = kn.astype(BF16)
        qs_ref[:, hs] = qn.astype(BF16)
        for d in range(2):
            c = d * HEADS + h
            beta = sig[:, 2 * HEADS + c:2 * HEADS + c + 1]
            egc = eg[:, c:c + 1]
            kb = kn * beta
            kb_ref[d, :, hs] = kb.astype(BF16)
            rw_ref[d, :, hs] = (kb * egc).astype(BF16)
            ru_ref[d, :, hs] = (v * beta).astype(BF16)
            kt_ref[d, :, hs] = (kn * ekt[:, c:c + 1]).astype(BF16)
            qd_ref[d, :, hs] = (qn * egc).astype(BF16)

    i64 = _iota((CHUNK, PAIR), 0)
    j64 = _iota((CHUNK, PAIR), 1) % CHUNK
    eye4 = (i64 == j64).astype(F32)
    diag2 = _iota((CHUNK, 2 * CHUNK), 0) == _iota((CHUNK, 2 * CHUNK), 1) % CHUNK
    BASE = 4
    blk = {}
    s = BASE
    while s <= CHUNK:
        blk[s] = (i64 // s) == (j64 // s)
        s *= 2

    def solve_chunks(it, carry):
        chains = []
        for cj in range(SOLVE_CHUNKS):
            rows = pl.ds(pl.multiple_of((it * SOLVE_CHUNKS + cj) * CHUNK, CHUNK), CHUNK)
            chains += [(rows, d, g) for d in range(2) for g in range(HEADS // 4)]

        def bd(x):
            return _blockdiag_rows(x.astype(BF16), CHUNK)

        ms = []
        for rows, d, g in chains:
            neg_l = []
            for p in range(2):
                pp = 2 * g + p
                ls = slice(pp * PAIR, (pp + 1) * PAIR)
                lhs = jnp.concatenate([kb_ref[d, rows, ls], qs_ref[rows, ls]], axis=0)
                gram = _dot_nt(lhs, _blockdiag_rows(kn_ref[rows, ls], HEAD_DIM))
                dm = dm_ref[rows, d * DCOLS + pp * 2 * CHUNK:d * DCOLS + (pp + 1) * 2 * CHUNK]
                neg_l.append(jnp.where(diag2, 0.0, -gram[:CHUNK] * dm))
                qkd_ref[d, rows, pp * 2 * CHUNK:(pp + 1) * 2 * CHUNK] = (gram[CHUNK:] * dm).astype(BF16)
            ms.append(jnp.concatenate(neg_l, axis=1))
        m4 = [jnp.where(blk[BASE], m, 0.0) for m in ms]
        m4sq = [_dot(x.astype(BF16), bd(x)) for x in m4]
        t_inv = [(eye4 + x) + _dot((eye4 + x).astype(BF16), bd(y)) for x, y in zip(m4, m4sq)]
        s = BASE
        while s < CHUNK:
            offd = blk[2 * s] & ~blk[s]
            ct = [_dot(jnp.where(offd, m, 0.0).astype(BF16), bd(t)) for m, t in zip(ms, t_inv)]
            t_inv = [t + _dot(t.astype(BF16), bd(c)) for t, c in zip(t_inv, ct)]
            s *= 2
        uws = []
        for (rows, d, g), t in zip(chains, t_inv):
            rhs = jnp.concatenate(
                [jnp.concatenate([ru_ref[d, rows, (4 * g + u) * HEAD_DIM:(4 * g + u + 1) * HEAD_DIM],
                                  rw_ref[d, rows, (4 * g + u) * HEAD_DIM:(4 * g + u + 1) * HEAD_DIM]], axis=1)
                 for u in range(4)], axis=0)
            uws.append(_dot(bd(t), rhs))
        for (rows, d, g), uw in zip(chains, uws):
            for u in range(4):
                hs = slice((4 * g + u) * HEAD_DIM, (4 * g + u + 1) * HEAD_DIM)
                u_ref[d, rows, hs] = uw[u * CHUNK:(u + 1) * CHUNK, :HEAD_DIM].astype(BF16)
                w_ref[d, rows, hs] = uw[u * CHUNK:(u + 1) * CHUNK, HEAD_DIM:].astype(BF16)
        return carry

    lax.fori_loop(0, ROWS // (CHUNK * SOLVE_CHUNKS), solve_chunks, 0)


def _prep(p_main, ab, conv_w, gate_params, tpb):
    n = p_main.shape[0]
    tile = lambda w: pl.BlockSpec((ROWS, w), lambda t: (t, 0))
    tile2 = lambda w: pl.BlockSpec((2, ROWS, w), lambda t: (0, t, 0))
    perdir = jax.ShapeDtypeStruct((2, n, DW), BF16)
    return pl.pallas_call(
        functools.partial(_prep_kernel, tpb),
        grid=(n // ROWS,),
        in_specs=[
            tile(3 * DW),
            tile(LANES),
            pl.BlockSpec((3, 3 * DW), lambda t: (0, 0)),
            pl.BlockSpec((8, LANES), lambda t: (0, 0)),
        ],
        out_specs=[tile2(DW), tile2(DW), tile2(DW), tile2(DW), tile2(DCOLS), tile(LANES)],
        out_shape=[perdir, perdir, perdir, perdir, jax.ShapeDtypeStruct((2, n, DCOLS), BF16),
                   jax.ShapeDtypeStruct((n, LANES), F32)],
        scratch_shapes=[pltpu.VMEM((ROWS, DW), BF16), pltpu.VMEM((ROWS, DW), BF16),
                        pltpu.VMEM((2, ROWS, DW), BF16), pltpu.VMEM((2, ROWS, DW), BF16),
                        pltpu.VMEM((2, ROWS, DW), BF16), pltpu.VMEM((ROWS, 2 * DCOLS), F32)],
        compiler_params=_cparams("parallel"),
        name="delta_prep",
    )(p_main, ab, conv_w, gate_params)


def _scan_kernel(uf, wf, ktf, qdf, qkf, gsf, ub, wb, ktb, qdb, qkb, gsb, of_ref, ob_ref, sf_ref, sb_ref):
    @pl.when(pl.program_id(1) == 0)
    def _():
        sf_ref[...] = jnp.zeros_like(sf_ref)
        sb_ref[...] = jnp.zeros_like(sb_ref)

    dirs = ((uf, wf, ktf, qdf, qkf, gsf, of_ref, sf_ref), (ub, wb, ktb, qdb, qkb, gsb, ob_ref, sb_ref))
    lane_p = _iota((8, PAIR), 1)
    bd_mask = (_iota((PAIR, PAIR), 0) // HEAD_DIM) == (_iota((PAIR, PAIR), 1) // HEAD_DIM)
    n_chunks = ROWS // CHUNK
    chains = [(d, pp) for d in range(2) for pp in range(HEADS // 2)]
    for step in range(n_chunks):
        def rows_of(d):
            ck = step if d == 0 else n_chunks - 1 - step
            return slice(ck * CHUNK, (ck + 1) * CHUNK)

        s_old, wq_s, v_new_b = [], [], []
        for d, pp in chains:
            u_ref, w_ref, kt_ref, qd_ref, qkd_ref, gs_ref, o_ref, s_ref = dirs[d]
            rows, ls = rows_of(d), slice(pp * PAIR, (pp + 1) * PAIR)
            s_old.append(s_ref[pp])
            wq_s.append(_dot(jnp.concatenate([w_ref[rows, ls], qd_ref[rows, ls]], axis=0), s_old[-1].astype(BF16)))
        for i, (d, pp) in enumerate(chains):
            u_ref = dirs[d][0]
            rows, ls = rows_of(d), slice(pp * PAIR, (pp + 1) * PAIR)
            v_new_b.append((u_ref[rows, ls].astype(F32) - wq_s[i][:CHUNK]).astype(BF16))
        for i, (d, pp) in enumerate(chains):
            u_ref, w_ref, kt_ref, qd_ref, qkd_ref, gs_ref, o_ref, s_ref = dirs[d]
            rows, ls = rows_of(d), slice(pp * PAIR, (pp + 1) * PAIR)
            qk = qkd_ref[rows, pp * 2 * CHUNK:(pp + 1) * 2 * CHUNK]
            o_ref[rows, ls] = (wq_s[i][CHUNK:] + _dot(qk, _blockdiag_rows(v_new_b[i], HEAD_DIM))).astype(BF16)
        for i, (d, pp) in enumerate(chains):
            u_ref, w_ref, kt_ref, qd_ref, qkd_ref, gs_ref, o_ref, s_ref = dirs[d]
            rows, ls = rows_of(d), slice(pp * PAIR, (pp + 1) * PAIR)
            gs = gs_ref[rows.start:rows.start + 8, :]
            c0 = d * HEADS + 2 * pp
            decay = jnp.where(lane_p < HEAD_DIM, gs[:, c0:c0 + 1], gs[:, c0 + 1:c0 + 2])
            upd = _dot_tn(kt_ref[rows, ls], v_new_b[i])
            s_dec = (s_old[i].reshape(PAIR // 8, 8, PAIR) * decay[None]).reshape(PAIR, PAIR)
            s_ref[pp] = s_dec + jnp.where(bd_mask, upd, 0.0)


def _scan(u, w, kt, qd, qkd, gs, bsz, n_lat_blocks, n_ctx_blocks):
    nb = n_lat_blocks + n_ctx_blocks
    fwd = lambda j: jnp.where(j < n_ctx_blocks, n_lat_blocks + j, j - n_ctx_blocks)
    bwd = lambda j: nb - 1 - j
    r5 = lambda a: a.reshape(2, bsz, nb, ROWS, a.shape[-1])

    def perdir(d, cmap, w):
        return pl.BlockSpec((None, None, None, ROWS, w), lambda b, j: (d, b, cmap(j), 0, 0))

    def shared(cmap, w):
        return pl.BlockSpec((None, None, ROWS, w), lambda b, j: (b, cmap(j), 0, 0))

    def specs(d, cmap):
        return [perdir(d, cmap, DW), perdir(d, cmap, DW), perdir(d, cmap, DW), perdir(d, cmap, DW),
                perdir(d, cmap, DCOLS), shared(cmap, LANES)]

    args = (r5(u), r5(w), r5(kt), r5(qd), r5(qkd), gs.reshape(bsz, nb, ROWS, LANES))
    o_shape = jax.ShapeDtypeStruct((bsz, nb, ROWS, DW), BF16)
    of, ob = pl.pallas_call(
        _scan_kernel,
        grid=(bsz, nb),
        in_specs=specs(0, fwd) + specs(1, bwd),
        out_specs=[shared(fwd, DW), shared(bwd, DW)],
        out_shape=[o_shape, o_shape],
        scratch_shapes=[pltpu.VMEM((HEADS // 2, PAIR, PAIR), F32), pltpu.VMEM((HEADS // 2, PAIR, PAIR), F32)],
        compiler_params=_cparams("parallel", "arbitrary"),
        name="delta_scan",
    )(*args, *args)
    return of.reshape(-1, DW), ob.reshape(-1, DW)


def _fourier_kernel(ct_ref, st_ref, uc_ref, us_ref, o_ref):
    o_ref[...] = (_dot(ct_ref[...], uc_ref[...]) - _dot(st_ref[...], us_ref[...])).astype(BF16)


def _fourier(p3, ct, st):
    bsz, tt, _ = p3.shape
    fw = 512
    return pl.pallas_call(
        _fourier_kernel,
        grid=(bsz, tt // ROWS),
        in_specs=[
            pl.BlockSpec((ROWS, tt), lambda b, i: (i, 0)),
            pl.BlockSpec((ROWS, tt), lambda b, i: (i, 0)),
            pl.BlockSpec((None, tt, fw), lambda b, i: (b, 0, COL_FC // fw)),
            pl.BlockSpec((None, tt, fw), lambda b, i: (b, 0, COL_FS // fw)),
        ],
        out_specs=pl.BlockSpec((None, ROWS, fw), lambda b, i: (b, i, 0)),
        out_shape=jax.ShapeDtypeStruct((bsz, tt, fw), BF16),
        compiler_params=_cparams("parallel", "arbitrary"),
        name="fourier_mix",
    )(ct, st, p3, p3)


def _merge_kernel(of_ref, ob_ref, z_ref, gt_ref, fm_ref, x_ref, mod_ref, on_ref, nf_ref,
                  wf_ref, wd_ref, wo_ref, wrh_ref, wrl_ref, br_ref, xo_ref, h2_ref, rt_ref, cnt_out_ref, cnt_ref):
    m = mod_ref[0]
    o = of_ref[...].astype(F32) + ob_ref[...].astype(F32)
    on = on_ref[...]
    parts = []
    for h in range(HEADS):
        hs = slice(h * HEAD_DIM, (h + 1) * HEAD_DIM)
        oh = o[:, hs]
        z = z_ref[:, hs].astype(F32)
        y = oh * lax.rsqrt(jnp.mean(oh * oh, axis=-1, keepdims=True) + EPS) * on
        parts.append((y * (z * _sigmoid(z))).astype(BF16))
    od = jnp.concatenate(parts, axis=1)
    pa = _dot(fm_ref[...], wf_ref[...])
    pb = _dot(od, wd_ref[...])
    d = pa.shape[1]
    ga = _sigmoid(gt_ref[:, :d].astype(F32))
    gb = _sigmoid(gt_ref[:, d:].astype(F32))
    y = _dot((ga * pa + gb * pb).astype(BF16), wo_ref[...])
    xn = x_ref[...] + m[2:3] * y
    xo_ref[...] = xn
    h2 = _norm_mod(xn, nf_ref[...], m[3:4], m[4:5])
    h2_ref[...] = h2

    a, b, _ = _split3(h2)
    lg = _dot(a, wrh_ref[...]) + _dot(a, wrl_ref[...]) + _dot(b, wrh_ref[...]) + br_ref[...]
    lane = _iota(lg.shape, 1).astype(F32)
    big = jnp.float32(1 << 20)
    ninf = jnp.float32(-jnp.inf)
    glog = jnp.where(lane < N_GROUPS, lg, ninf)
    gmax = jnp.max(glog, axis=-1, keepdims=True)
    grp = jnp.min(jnp.where(glog == gmax, lane, big), axis=-1, keepdims=True)
    p_grp = 1.0 / jnp.sum(jnp.exp(glog - gmax), axis=-1, keepdims=True)
    lo = N_GROUPS + grp * EXPERTS_PER_GROUP
    el = jnp.where((lane >= lo) & (lane < lo + EXPERTS_PER_GROUP), lg, ninf)
    v1 = jnp.max(el, axis=-1, keepdims=True)
    i1 = jnp.min(jnp.where(el == v1, lane, big), axis=-1, keepdims=True)
    el2 = jnp.where(lane == i1, ninf, el)
    v2 = jnp.max(el2, axis=-1, keepdims=True)
    i2 = jnp.min(jnp.where(el2 == v2, lane, big), axis=-1, keepdims=True)
    e2 = jnp.exp(v2 - v1)
    w1 = p_grp / (1.0 + e2)
    w2 = p_grp * e2 / (1.0 + e2)
    @pl.when(pl.program_id(0) == 0)
    def _():
        cnt_ref[...] = jnp.zeros_like(cnt_ref)

    rows = lg.shape[0]
    onehot = jnp.where((lane == i1) | (lane == i2), 1.0, 0.0).astype(BF16)
    earlier = (_iota((rows, rows), 1) < _iota((rows, rows), 0)).astype(BF16)
    before = _dot(earlier, onehot) + cnt_ref[0:1, :]
    rank1 = jnp.sum(jnp.where(lane == i1, before, 0.0), axis=-1, keepdims=True)
    rank2 = jnp.sum(jnp.where(lane == i2, before, 0.0), axis=-1, keepdims=True)
    cnt_ref[...] = cnt_ref[...] + _dot(jnp.ones((8, rows), BF16), onehot)
    cnt_out_ref[...] = cnt_ref[...]
    rt = jnp.where(lane == 0, i1 - N_GROUPS,
                   jnp.where(lane == 1, i2 - N_GROUPS,
                             jnp.where(lane == 2, w1,
                                       jnp.where(lane == 3, w2,
                                                 jnp.where(lane == 4, rank1, jnp.where(lane == 5, rank2, 0.0))))))
    rt_ref[...] = rt


def _merge(of, ob, p_main, fm, x2, mod, on_t, nf, wf, wd, wo, wr, br, tpb, bsz):
    wr_hi = wr.astype(BF16)
    wr_lo = (wr - wr_hi.astype(F32)).astype(BF16)
    n, d = x2.shape
    tile = lambda w, c=0: pl.BlockSpec((ROWS, w), lambda t: (t, c))
    full = lambda a: pl.BlockSpec(a.shape, lambda t: (0,) * a.ndim)
    return pl.pallas_call(
        _merge_kernel,
        grid=(n // ROWS,),
        in_specs=[
            tile(DW), tile(DW),
            tile(DW, COL_Z // DW),
            tile(2 * d, COL_G // (2 * d)),
            tile(fm.shape[1]),
            tile(d),
            pl.BlockSpec((1, N_MOD, d), _mod_row(tpb, bsz)),
            full(on_t), full(nf), full(wf), full(wd), full(wo), full(wr_hi), full(wr_lo), full(br),
        ],
        out_specs=[tile(d), tile(d), tile(LANES), pl.BlockSpec((8, LANES), lambda t: (0, 0))],
        out_shape=[jax.ShapeDtypeStruct((n, d), F32), jax.ShapeDtypeStruct((n, d), F32),
                   jax.ShapeDtypeStruct((n, LANES), F32), jax.ShapeDtypeStruct((8, LANES), F32)],
        scratch_shapes=[pltpu.VMEM((8, LANES), F32)],
        compiler_params=_cparams("arbitrary"),
        name="merge_route",
    )(of, ob, p_main, p_main, fm, x2, mod, on_t, nf, wf, wd, wo, wr_hi, wr_lo, br)


def _dispatch_kernel(slot_ref, h_ref, xs_in_ref, xs_ref, sem):
    del xs_in_ref

    def copy(r, k):
        return pltpu.make_async_copy(h_ref.at[pl.ds(r, 1)], xs_ref.at[pl.ds(slot_ref[k, r], 1)], sem)

    def start(r8, carry):
        for j in range(8):
            for k in range(TOP_K):
                copy(r8 * 8 + j, k).start()
        return carry

    lax.fori_loop(0, ROWS // 8, start, 0)
    for r in range(ROWS):
        for k in range(TOP_K):
            copy(r, k).wait()


def _dispatch(slots, h2, n_slots):
    n, d = h2.shape
    return pl.pallas_call(
        _dispatch_kernel,
        grid=(n // ROWS,),
        in_specs=[
            pl.BlockSpec((None, TOP_K, ROWS), lambda t: (t, 0, 0), memory_space=pltpu.SMEM),
            pl.BlockSpec((ROWS, d), lambda t: (t, 0)),
            pl.BlockSpec(memory_space=pl.ANY),
        ],
        out_specs=pl.BlockSpec(memory_space=pl.ANY),
        out_shape=jax.ShapeDtypeStruct((n_slots, d), F32),
        scratch_shapes=[pltpu.SemaphoreType.DMA(())],
        input_output_aliases={2: 0},
        compiler_params=_cparams("arbitrary"),
        name="moe_dispatch",
    )(slots, h2, jnp.zeros((n_slots, d), F32))


def _expert_kernel(be_ref, nu_ref, x_ref, wg_ref, wu_ref, wd_ref, y_ref, wgb_ref, wub_ref, wdb_ref):
    i = pl.program_id(0)

    @pl.when((i == 0) | (be_ref[i] != be_ref[jnp.maximum(i - 1, 0)]))
    def _():
        wgb_ref[...] = wg_ref[...].astype(BF16)
        wub_ref[...] = wu_ref[...].astype(BF16)
        wdb_ref[...] = wd_ref[...].astype(BF16)

    @pl.when(i < nu_ref[0])
    def _():
        x = x_ref[...].astype(BF16)
        a = _dot(x, wgb_ref[...])
        u = _dot(x, wub_ref[...])
        hmid = (a * _sigmoid(a) * u).astype(BF16)
        y_ref[...] = _dot(hmid, wdb_ref[...])

    @pl.when(i >= nu_ref[0])
    def _():
        y_ref[...] = jnp.zeros_like(y_ref)


def _experts(blk_e, n_used, xs, wg, wu, wd, layer):
    n_slots, d = xs.shape
    de = wg.shape[-1]
    grid_spec = pltpu.PrefetchScalarGridSpec(
        num_scalar_prefetch=2,
        grid=(n_slots // MOE_BLOCK,),
        in_specs=[
            pl.BlockSpec((MOE_BLOCK, d), lambda i, be, nu: (i, 0)),
            pl.BlockSpec((None, None, d, de), lambda i, be, nu: (layer, be[i], 0, 0)),
            pl.BlockSpec((None, None, d, de), lambda i, be, nu: (layer, be[i], 0, 0)),
            pl.BlockSpec((None, None, de, d), lambda i, be, nu: (layer, be[i], 0, 0)),
        ],
        out_specs=pl.BlockSpec((MOE_BLOCK, d), lambda i, be, nu: (i, 0)),
        scratch_shapes=[pltpu.VMEM((d, de), BF16), pltpu.VMEM((d, de), BF16), pltpu.VMEM((de, d), BF16)],
    )
    return pl.pallas_call(
        _expert_kernel,
        grid_spec=grid_spec,
        out_shape=jax.ShapeDtypeStruct((n_slots, d), F32),
        compiler_params=_cparams("arbitrary"),
        name="moe_experts",
    )(blk_e, n_used, xs, wg, wu, wd)


def _combine_kernel(x_ref, y0_ref, y1_ref, rt_ref, mod_ref, o_ref):
    m = mod_ref[0]
    rt = rt_ref[...]
    y = rt[:, 2:3] * y0_ref[...] + rt[:, 3:4] * y1_ref[...]
    o_ref[...] = x_ref[...] + m[5:6] * y


def _combine(x2, y0, y1, rt, mod, tpb, bsz):
    n, d = x2.shape
    tile = lambda w: pl.BlockSpec((ROWS, w), lambda t: (t, 0))
    return pl.pallas_call(
        _combine_kernel,
        grid=(n // ROWS,),
        in_specs=[tile(d), tile(d), tile(d), tile(LANES), pl.BlockSpec((1, N_MOD, d), _mod_row(tpb, bsz))],
        out_specs=tile(d),
        out_shape=jax.ShapeDtypeStruct((n, d), F32),
        compiler_params=_cparams("parallel"),
        name="moe_combine",
    )(x2, y0, y1, rt, mod)


def _route_tables(rt, counts_f, n):
    counts = counts_f[0, N_GROUPS:N_GROUPS + N_EXPERTS].astype(jnp.int32)
    padded = (counts + MOE_BLOCK - 1) // MOE_BLOCK * MOE_BLOCK
    pad_end = jnp.cumsum(padded)
    pad_start = pad_end - padded
    eid = rt[:, 0:TOP_K].astype(jnp.int32)
    rank = rt[:, 4:4 + TOP_K].astype(jnp.int32)
    onehot = eid[:, :, None] == jnp.arange(N_EXPERTS, dtype=jnp.int32)
    slot_of = jnp.sum(jnp.where(onehot, pad_start, 0), axis=-1) + rank
    n_slots = -(-n * TOP_K // MOE_BLOCK) * MOE_BLOCK + N_EXPERTS * MOE_BLOCK
    blk_start = jnp.arange(n_slots // MOE_BLOCK, dtype=jnp.int32) * MOE_BLOCK
    blk_e = jnp.minimum(jnp.sum(blk_start[:, None] >= pad_end[None, :], axis=-1), N_EXPERTS - 1).astype(jnp.int32)
    n_used = (pad_end[-1] // MOE_BLOCK).astype(jnp.int32).reshape(1)
    return slot_of, blk_e, n_used, n_slots


def _final_kernel(x_ref, w_ref, o_ref):
    x = x_ref[...]
    o_ref[...] = x * lax.rsqrt(jnp.mean(x * x, axis=-1, keepdims=True) + EPS) * w_ref[...]


def _final_norm(x3, w, seq):
    bsz, _, d = x3.shape
    return pl.pallas_call(
        _final_kernel,
        grid=(bsz, seq // ROWS),
        in_specs=[pl.BlockSpec((None, ROWS, d), lambda b, i: (b, i, 0)), pl.BlockSpec((1, d), lambda b, i: (0, 0))],
        out_specs=pl.BlockSpec((None, ROWS, d), lambda b, i: (b, i, 0)),
        out_shape=jax.ShapeDtypeStruct((bsz, seq, d), F32),
        compiler_params=_cparams("parallel", "parallel"),
        name="final_norm",
    )(x3, w)


def _dft_tables(seq, ctx_len, group_dim):
    tt = seq + ctx_len
    r = jnp.arange(tt, dtype=jnp.int32)[:, None]
    c = jnp.arange(tt, dtype=jnp.int32)[None, :]
    lat = (r < seq) & (c < seq)
    ctx = (r >= seq) & (c >= seq)
    k_lat = (r * c) % seq
    k_ctx = ((r - seq) * (c - seq)) % ctx_len
    ang = jnp.where(lat, k_lat.astype(F32) * (2.0 * jnp.pi / seq), k_ctx.astype(F32) * (2.0 * jnp.pi / ctx_len))
    scale = jnp.where(lat, (seq * group_dim) ** -0.5, jnp.where(ctx, (ctx_len * group_dim) ** -0.5, 0.0))
    return (jnp.cos(ang) * scale).astype(BF16), (jnp.sin(ang) * scale).astype(BF16)


def _channel_dft(group_dim):
    k = (jnp.arange(group_dim, dtype=jnp.int32)[:, None] * jnp.arange(group_dim, dtype=jnp.int32)[None, :]) % group_dim
    ang = k.astype(F32) * (2.0 * jnp.pi / group_dim)
    eye = jnp.eye(FOURIER_GROUPS, dtype=F32)
    return jnp.kron(eye, jnp.cos(ang)), jnp.kron(eye, jnp.sin(ang))


def kernel(x, c, ctx, c_ctx, w_mod, b_mod, norm_mix, norm_ffn, w_in, conv_w, a_log, dt_bias, out_norm,
           w_fourier, w_delta, w_out, w_route_group, b_route_group, w_route_expert, b_route_expert,
           w_gate, w_up, w_down, final_norm):
    bsz, seq, d = x.shape
    ctx_len = ctx.shape[1]
    depth = w_mod.shape[0]
    assert d == HEADS * HEAD_DIM and ctx_len == ROWS and seq % ROWS == 0 and bsz < 16
    tt = seq + ctx_len
    tpb = tt // ROWS
    n = bsz * tt
    fw = w_fourier.shape[1]
    group_dim = fw // FOURIER_GROUPS

    cvec = jnp.zeros((16, d), F32).at[:bsz].set(c).at[bsz].set(c_ctx)
    mod_all = _modulation(cvec, w_mod, b_mod).reshape(depth, 16, N_MOD, d)
    ct, st = _dft_tables(seq, ctx_len, group_dim)
    cc, sc = _channel_dft(group_dim)

    o_a = 3 * DW
    o_b = o_a + 2 * HEADS
    o_z = o_b + 2 * HEADS
    o_f = o_z + DW
    o_g = o_f + fw

    xcur = jnp.concatenate([x, ctx], axis=1).reshape(n, d)
    for i in range(depth):
        wi = w_in[i]
        wfold_c = jnp.dot(wi[:, o_f:o_g], cc, precision=lax.Precision.HIGHEST)
        wfold_s = jnp.dot(wi[:, o_f:o_g], sc, precision=lax.Precision.HIGHEST)
        w_main = jnp.concatenate([wi[:, :o_a], wi[:, o_z:o_f], wi[:, o_g:], wfold_c, wfold_s], axis=1).astype(BF16)
        w_ab = jnp.concatenate([wi[:, o_a:o_z], jnp.zeros((d, LANES - 4 * HEADS), F32)], axis=1).astype(BF16)
        gate_params = jnp.zeros((8, LANES), F32)
        gate_params = gate_params.at[0, :2 * HEADS].set(a_log[i].reshape(-1)).at[1, :2 * HEADS].set(dt_bias[i].reshape(-1))
        mod = mod_all[i]

        p_main, ab = _inproj(xcur, mod, norm_mix[i].reshape(1, d), w_main, w_ab, tpb, bsz)
        u, w, kt, qd, qkd, gs = _prep(p_main, ab, conv_w[i], gate_params, tpb)
        of, ob = _scan(u, w, kt, qd, qkd, gs, bsz, seq // ROWS, ctx_len // ROWS)
        fm = _fourier(p_main.reshape(bsz, tt, MAIN_COLS), ct, st).reshape(n, fw)

        wr = jnp.zeros((d, LANES), F32).at[:, :N_GROUPS].set(w_route_group[i])
        wr = wr.at[:, N_GROUPS:N_GROUPS + N_EXPERTS].set(w_route_expert[i])
        br = jnp.zeros((1, LANES), F32).at[0, :N_GROUPS].set(b_route_group[i])
        br = br.at[0, N_GROUPS:N_GROUPS + N_EXPERTS].set(b_route_expert[i])
        xcur, h2, rt, counts = _merge(of, ob, p_main, fm, xcur, mod, out_norm[i].reshape(1, HEAD_DIM),
                                      norm_ffn[i].reshape(1, d), w_fourier[i].astype(BF16),
                                      w_delta[i].astype(BF16), w_out[i].astype(BF16), wr, br, tpb, bsz)

        slot_of, blk_e, n_used, n_slots = _route_tables(rt, counts, n)
        slots = jnp.swapaxes(slot_of.reshape(n // ROWS, ROWS, TOP_K), 1, 2)
        xs = _dispatch(slots, h2, n_slots)
        ys = _experts(blk_e, n_used, xs, w_gate, w_up, w_down, i)
        xcur = _combine(xcur, ys[slot_of[:, 0]], ys[slot_of[:, 1]], rt, mod, tpb, bsz)

    return _final_norm(xcur.reshape(bsz, tt, d), final_norm.reshape(1, d), seq)
```

```python
import functools

import jax
import jax.numpy as jnp
from jax import lax
from jax.experimental import pallas as pl
from jax.experimental.pallas import tpu as pltpu

F32 = jnp.float32
BF16 = jnp.bfloat16

EPS = 1e-6
ROWS = 256
CHUNK = 64
HEADS = 8
HEAD_DIM = 128
PAIR = 2 * HEAD_DIM
N_MOD = 6
N_GROUPS = 4
EXPERTS_PER_GROUP = 8
N_EXPERTS = N_GROUPS * EXPERTS_PER_GROUP
TOP_K = 2
MOE_BLOCK = 256
FOURIER_GROUPS = 4
LANES = 128
VMEM_LIMIT = 56 * 1024 * 1024


def _cparams(*sem):
    return pltpu.CompilerParams(dimension_semantics=sem, vmem_limit_bytes=VMEM_LIMIT)


def _sigmoid(x):
    return 1.0 / (1.0 + jnp.exp(-x))


def _dot(a, b):
    return jnp.dot(a, b, preferred_element_type=F32)


def _dot_nt(a, b):
    return lax.dot_general(a, b, (((1,), (1,)), ((), ())), preferred_element_type=F32)


def _dot_tn(a, b):
    return lax.dot_general(a, b, (((0,), (0,)), ((), ())), preferred_element_type=F32)


def _split3(x):
    h1 = x.astype(BF16)
    r1 = x - h1.astype(F32)
    h2 = r1.astype(BF16)
    r2 = r1 - h2.astype(F32)
    return h1, h2, r2.astype(BF16)


def _dot_x01(x, m01):
    a, b, c = _split3(x)
    return _dot(a, m01) + _dot(b, m01) + _dot(c, m01)


def _dot_01x(m01, x):
    a, b, c = _split3(x)
    return _dot(m01, a) + _dot(m01, b) + _dot(m01, c)


def _iota(shape, dim):
    return lax.broadcasted_iota(jnp.int32, shape, dim)


def _mod_kernel(c_ref, w_ref, b_ref, o_ref):
    c = c_ref[...]
    s = c * _sigmoid(c)
    w = w_ref[0]
    a, b, _ = _split3(s)
    wa, wb, _ = _split3(w)
    o_ref[0] = _dot(a, wa) + _dot(a, wb) + _dot(b, wa) + b_ref[0]


def _modulation(cvec, w_mod, b_mod):
    depth, d, n = w_mod.shape
    tn = 1536
    return pl.pallas_call(
        _mod_kernel,
        grid=(depth, n // tn),
        in_specs=[
            pl.BlockSpec((16, d), lambda i, j: (0, 0)),
            pl.BlockSpec((1, d, tn), lambda i, j: (i, 0, j)),
            pl.BlockSpec((1, 1, tn), lambda i, j: (i, 0, j)),
        ],
        out_specs=pl.BlockSpec((1, 16, tn), lambda i, j: (i, 0, j)),
        out_shape=jax.ShapeDtypeStruct((depth, 16, n), F32),
        compiler_params=_cparams("arbitrary", "arbitrary"),
        name="modulation",
    )(cvec, w_mod, b_mod.reshape(depth, 1, n))


COL_QKV = 0
COL_Z = 3 * HEADS * HEAD_DIM
COL_G = COL_Z + HEADS * HEAD_DIM
COL_FC = COL_G + 2 * 1024
COL_FS = COL_FC + 512
MAIN_COLS = COL_FS + 512
COL_TILE = 512


def _norm_mod(x, nw, shift, scale):
    ms = jnp.mean(x * x, axis=-1, keepdims=True)
    return (x * lax.rsqrt(ms + EPS) * nw) * (1.0 + scale) + shift


def _inproj_kernel(x_ref, mod_ref, nw_ref, w_ref, wab_ref, p_ref, ab_ref):
    m = mod_ref[0]
    h = _norm_mod(x_ref[...], nw_ref[...], m[0:1], m[1:2]).astype(BF16)
    for c in range(MAIN_COLS // COL_TILE):
        cs = slice(c * COL_TILE, (c + 1) * COL_TILE)
        p_ref[:, cs] = _dot(h, w_ref[:, cs]).astype(BF16)
    ab_ref[...] = _dot(h, wab_ref[...])


def _mod_row(tpb, bsz):
    return lambda t: (jnp.where(t % tpb == tpb - 1, bsz, t // tpb), 0, 0)


def _inproj(x2, mod, nw, w_main, w_ab, tpb, bsz):
    n, d = x2.shape
    return pl.pallas_call(
        _inproj_kernel,
        grid=(n // ROWS,),
        in_specs=[
            pl.BlockSpec((ROWS, d), lambda t: (t, 0)),
            pl.BlockSpec((1, N_MOD, d), _mod_row(tpb, bsz)),
            pl.BlockSpec((1, d), lambda t: (0, 0)),
            pl.BlockSpec((d, MAIN_COLS), lambda t: (0, 0)),
            pl.BlockSpec((d, LANES), lambda t: (0, 0)),
        ],
        out_specs=[
            pl.BlockSpec((ROWS, MAIN_COLS), lambda t: (t, 0)),
            pl.BlockSpec((ROWS, LANES), lambda t: (t, 0)),
        ],
        out_shape=[
            jax.ShapeDtypeStruct((n, MAIN_COLS), BF16),
            jax.ShapeDtypeStruct((n, LANES), F32),
        ],
        compiler_params=_cparams("parallel"),
        name="inproj",
    )(x2, mod, nw, w_main, w_ab)


DW = HEADS * HEAD_DIM
DCOLS = HEADS * CHUNK
SOLVE_CHUNKS = 4


def _blockdiag_rows(x, width):
    lane_blk = _iota(x.shape, 1) // width
    zero = jnp.zeros_like(x)
    return jnp.concatenate([jnp.where(lane_blk == u, x, zero) for u in range(x.shape[1] // width)], axis=0)


def _prep_kernel(tpb, qkv_ref, ab_ref, cw_ref, gp_ref,
                 u_ref, w_ref, kt_ref, qd_ref, qkd_ref, gs_ref,
                 kn_ref, qs_ref, kb_ref, rw_ref, ru_ref, dm_ref):
    t = pl.program_id(0)
    is_ctx = (t % tpb) == (tpb - 1)
    r = _iota((ROWS, 1), 0)
    rp = jnp.where(is_ctx, r, r % CHUNK)
    has_prev = rp != 0
    has_next = rp != jnp.where(is_ctx, ROWS - 1, CHUNK - 1)

    ab = ab_ref[...]
    gp = gp_ref[...]
    xg = ab + gp[1:2]
    softplus = jnp.maximum(xg, 0.0) + jnp.log(1.0 + jnp.exp(-jnp.abs(xg)))
    g = -jnp.exp(gp[0:1]) * softplus
    sig = _sigmoid(ab)

    ri = _iota((ROWS, ROWS), 0)
    ci = _iota((ROWS, ROWS), 1)
    same = (ri // CHUNK) == (ci // CHUNK)
    tri_lo = (same & (ci <= ri)).astype(BF16)
    tri_up = (same & (ci >= ri)).astype(BF16)
    ones_bd = same.astype(BF16)
    lane = _iota((ROWS, LANES), 1)
    gc = jnp.where(lane < HEADS, _dot_01x(tri_lo, g), _dot_01x(tri_up, g))
    tot = _dot_01x(ones_bd, g)
    eg = jnp.exp(gc)
    ekt = jnp.exp(tot - gc)
    gs_ref[...] = jnp.exp(tot)

    er = _iota((LANES, DCOLS), 0)
    ec = _iota((LANES, DCOLS), 1)
    di = _iota((ROWS, DCOLS), 0) % CHUNK
    dj = _iota((ROWS, DCOLS), 1) % CHUNK
    for d in range(2):
        expand = (er - d * HEADS == ec // CHUNK).astype(BF16)
        gcol = _dot_x01(gc, expand)
        grow = _dot_01x(ones_bd, jnp.where(di == dj, gcol, 0.0))
        keep = (dj <= di) if d == 0 else (dj >= di)
        dm_ref[:, d * DCOLS:(d + 1) * DCOLS] = jnp.where(keep, jnp.exp(jnp.where(keep, gcol - grow, 0.0)), 0.0)

    cw = cw_ref[...]

    def conv_silu(col):
        cs = slice(col * HEAD_DIM, (col + 1) * HEAD_DIM)
        x = qkv_ref[:, cs].astype(F32)
        prev = jnp.where(has_prev, pltpu.roll(x, 1, 0), 0.0)
        nxt = jnp.where(has_next, pltpu.roll(x, ROWS - 1, 0), 0.0)
        y = cw[1:2, cs] * x + cw[0:1, cs] * prev + cw[2:3, cs] * nxt
        return y * _sigmoid(y)

    for h in range(HEADS):
        hs = slice(h * HEAD_DIM, (h + 1) * HEAD_DIM)
        q = conv_silu(h)
        k = conv_silu(HEADS + h)
        v = conv_silu(2 * HEADS + h)
        qn = q * lax.rsqrt(jnp.sum(q * q, axis=-1, keepdims=True) + EPS) * (HEAD_DIM ** -0.5)
        kn = k * lax.rsqrt(jnp.sum(k * k, axis=-1, keepdims=True) + EPS)
        kn_ref[:, hs] = kn.astype(BF16)
        qs_ref[:, hs] = qn.astype(BF16)
        for d in range(2):
            c = d * HEADS + h
            beta = sig[:, 2 * HEADS + c:2 * HEADS + c + 1]
            egc = eg[:, c:c + 1]
            kb = kn * beta
            kb_ref[d, :, hs] = kb.astype(BF16)
            rw_ref[d, :, hs] = (kb * egc).astype(BF16)
            ru_ref[d, :, hs] = (v * beta).astype(BF16)
            kt_ref[d, :, hs] = (kn * ekt[:, c:c + 1]).astype(BF16)
            qd_ref[d, :, hs] = (qn * egc).astype(BF16)

    i64 = _iota((CHUNK, PAIR), 0)
    j64 = _iota((CHUNK, PAIR), 1) % CHUNK
    eye4 = (i64 == j64).astype(F32)
    diag2 = _iota((CHUNK, 2 * CHUNK), 0) == _iota((CHUNK, 2 * CHUNK), 1) % CHUNK
    BASE = 4
    blk = {}
    s = BASE
    while s <= CHUNK:
        blk[s] = (i64 // s) == (j64 // s)
        s *= 2

    def solve_chunks(it, carry):
        chains = []
        for cj in range(SOLVE_CHUNKS):
            rows = pl.ds(pl.multiple_of((it * SOLVE_CHUNKS + cj) * CHUNK, CHUNK), CHUNK)
            chains += [(rows, d, g) for d in range(2) for g in range(HEADS // 4)]

        def bd(x):
            return _blockdiag_rows(x.astype(BF16), CHUNK)

        ms = []
        for rows, d, g in chains:
            neg_l = []
            for p in range(2):
                pp = 2 * g + p
                ls = slice(pp * PAIR, (pp + 1) * PAIR)
                lhs = jnp.concatenate([kb_ref[d, rows, ls], qs_ref[rows, ls]], axis=0)
                gram = _dot_nt(lhs, _blockdiag_rows(kn_ref[rows, ls], HEAD_DIM))
                dm = dm_ref[rows, d * DCOLS + pp * 2 * CHUNK:d * DCOLS + (pp + 1) * 2 * CHUNK]
                neg_l.append(jnp.where(diag2, 0.0, -gram[:CHUNK] * dm))
                qkd_ref[d, rows, pp * 2 * CHUNK:(pp + 1) * 2 * CHUNK] = (gram[CHUNK:] * dm).astype(BF16)
            ms.append(jnp.concatenate(neg_l, axis=1))
        m4 = [jnp.where(blk[BASE], m, 0.0) for m in ms]
        m4sq = [_dot(x.astype(BF16), bd(x)) for x in m4]
        t_inv = [(eye4 + x) + _dot((eye4 + x).astype(BF16), bd(y)) for x, y in zip(m4, m4sq)]
        s = BASE
        while s < CHUNK:
            offd = blk[2 * s] & ~blk[s]
            ct = [_dot(jnp.where(offd, m, 0.0).astype(BF16), bd(t)) for m, t in zip(ms, t_inv)]
            t_inv = [t + _dot(t.astype(BF16), bd(c)) for t, c in zip(t_inv, ct)]
            s *= 2
        uws = []
        for (rows, d, g), t in zip(chains, t_inv):
            rhs = jnp.concatenate(
                [jnp.concatenate([ru_ref[d, rows, (4 * g + u) * HEAD_DIM:(4 * g + u + 1) * HEAD_DIM],
                                  rw_ref[d, rows, (4 * g + u) * HEAD_DIM:(4 * g + u + 1) * HEAD_DIM]], axis=1)
                 for u in range(4)], axis=0)
            uws.append(_dot(bd(t), rhs))
        for (rows, d, g), uw in zip(chains, uws):
            for u in range(4):
                hs = slice((4 * g + u) * HEAD_DIM, (4 * g + u + 1) * HEAD_DIM)
                u_ref[d, rows, hs] = uw[u * CHUNK:(u + 1) * CHUNK, :HEAD_DIM].astype(BF16)
                w_ref[d, rows, hs] = uw[u * CHUNK:(u + 1) * CHUNK, HEAD_DIM:].astype(BF16)
        return carry

    lax.fori_loop(0, ROWS // (CHUNK * SOLVE_CHUNKS), solve_chunks, 0)


def _prep(p_main, ab, conv_w, gate_params, tpb):
    n = p_main.shape[0]
    tile = lambda w: pl.BlockSpec((ROWS, w), lambda t: (t, 0))
    tile2 = lambda w: pl.BlockSpec((2, ROWS, w), lambda t: (0, t, 0))
    perdir = jax.ShapeDtypeStruct((2, n, DW), BF16)
    return pl.pallas_call(
        functools.partial(_prep_kernel, tpb),
        grid=(n // ROWS,),
        in_specs=[
            tile(3 * DW),
            tile(LANES),
            pl.BlockSpec((3, 3 * DW), lambda t: (0, 0)),
            pl.BlockSpec((8, LANES), lambda t: (0, 0)),
        ],
        out_specs=[tile2(DW), tile2(DW), tile2(DW), tile2(DW), tile2(DCOLS), tile(LANES)],
        out_shape=[perdir, perdir, perdir, perdir, jax.ShapeDtypeStruct((2, n, DCOLS), BF16),
                   jax.ShapeDtypeStruct((n, LANES), F32)],
        scratch_shapes=[pltpu.VMEM((ROWS, DW), BF16), pltpu.VMEM((ROWS, DW), BF16),
                        pltpu.VMEM((2, ROWS, DW), BF16), pltpu.VMEM((2, ROWS, DW), BF16),
                        pltpu.VMEM((2, ROWS, DW), BF16), pltpu.VMEM((ROWS, 2 * DCOLS), F32)],
        compiler_params=_cparams("parallel"),
        name="delta_prep",
    )(p_main, ab, conv_w, gate_params)


def _scan_kernel(uf, wf, ktf, qdf, qkf, gsf, ub, wb, ktb, qdb, qkb, gsb, of_ref, ob_ref, sf_ref, sb_ref):
    @pl.when(pl.program_id(1) == 0)
    def _():
        sf_ref[...] = jnp.zeros_like(sf_ref)
        sb_ref[...] = jnp.zeros_like(sb_ref)

    dirs = ((uf, wf, ktf, qdf, qkf, gsf, of_ref, sf_ref), (ub, wb, ktb, qdb, qkb, gsb, ob_ref, sb_ref))
    lane_p = _iota((8, PAIR), 1)
    bd_mask = (_iota((PAIR, PAIR), 0) // HEAD_DIM) == (_iota((PAIR, PAIR), 1) // HEAD_DIM)
    n_chunks = ROWS // CHUNK
    chains = [(d, pp) for d in range(2) for pp in range(HEADS // 2)]
    for step in range(n_chunks):
        def rows_of(d):
            ck = step if d == 0 else n_chunks - 1 - step
            return slice(ck * CHUNK, (ck + 1) * CHUNK)

        s_old, wq_s, v_new_b = [], [], []
        for d, pp in chains:
            u_ref, w_ref, kt_ref, qd_ref, qkd_ref, gs_ref, o_ref, s_ref = dirs[d]
            rows, ls = rows_of(d), slice(pp * PAIR, (pp + 1) * PAIR)
            s_old.append(s_ref[pp])
            wq_s.append(_dot(jnp.concatenate([w_ref[rows, ls], qd_ref[rows, ls]], axis=0), s_old[-1].astype(BF16)))
        for i, (d, pp) in enumerate(chains):
            u_ref = dirs[d][0]
            rows, ls = rows_of(d), slice(pp * PAIR, (pp + 1) * PAIR)
            v_new_b.append((u_ref[rows, ls].astype(F32) - wq_s[i][:CHUNK]).astype(BF16))
        for i, (d, pp) in enumerate(chains):
            u_ref, w_ref, kt_ref, qd_ref, qkd_ref, gs_ref, o_ref, s_ref = dirs[d]
            rows, ls = rows_of(d), slice(pp * PAIR, (pp + 1) * PAIR)
            qk = qkd_ref[rows, pp * 2 * CHUNK:(pp + 1) * 2 * CHUNK]
            o_ref[rows, ls] = (wq_s[i][CHUNK:] + _dot(qk, _blockdiag_rows(v_new_b[i], HEAD_DIM))).astype(BF16)
        for i, (d, pp) in enumerate(chains):
            u_ref, w_ref, kt_ref, qd_ref, qkd_ref, gs_ref, o_ref, s_ref = dirs[d]
            rows, ls = rows_of(d), slice(pp * PAIR, (pp + 1) * PAIR)
            gs = gs_ref[rows.start:rows.start + 8, :]
            c0 = d * HEADS + 2 * pp
            decay = jnp.where(lane_p < HEAD_DIM, gs[:, c0:c0 + 1], gs[:, c0 + 1:c0 + 2])
            upd = _dot_tn(kt_ref[rows, ls], v_new_b[i])
            s_dec = (s_old[i].reshape(PAIR // 8, 8, PAIR) * decay[None]).reshape(PAIR, PAIR)
            s_ref[pp] = s_dec + jnp.where(bd_mask, upd, 0.0)


def _scan(u, w, kt, qd, qkd, gs, bsz, n_lat_blocks, n_ctx_blocks):
    nb = n_lat_blocks + n_ctx_blocks
    fwd = lambda j: jnp.where(j < n_ctx_blocks, n_lat_blocks + j, j - n_ctx_blocks)
    bwd = lambda j: nb - 1 - j
    r5 = lambda a: a.reshape(2, bsz, nb, ROWS, a.shape[-1])

    def perdir(d, cmap, w):
        return pl.BlockSpec((None, None, None, ROWS, w), lambda b, j: (d, b, cmap(j), 0, 0))

    def shared(cmap, w):
        return pl.BlockSpec((None, None, ROWS, w), lambda b, j: (b, cmap(j), 0, 0))

    def specs(d, cmap):
        return [perdir(d, cmap, DW), perdir(d, cmap, DW), perdir(d, cmap, DW), perdir(d, cmap, DW),
                perdir(d, cmap, DCOLS), shared(cmap, LANES)]

    args = (r5(u), r5(w), r5(kt), r5(qd), r5(qkd), gs.reshape(bsz, nb, ROWS, LANES))
    o_shape = jax.ShapeDtypeStruct((bsz, nb, ROWS, DW), BF16)
    of, ob = pl.pallas_call(
        _scan_kernel,
        grid=(bsz, nb),
        in_specs=specs(0, fwd) + specs(1, bwd),
        out_specs=[shared(fwd, DW), shared(bwd, DW)],
        out_shape=[o_shape, o_shape],
        scratch_shapes=[pltpu.VMEM((HEADS // 2, PAIR, PAIR), F32), pltpu.VMEM((HEADS // 2, PAIR, PAIR), F32)],
        compiler_params=_cparams("parallel", "arbitrary"),
        name="delta_scan",
    )(*args, *args)
    return of.reshape(-1, DW), ob.reshape(-1, DW)


XROWS = 16


def _fourier_lat_kernel(cm_ref, cx_ref, sm_ref, sx_ref, uc_ref, us_ref, lo_ref, hi_ref):
    ct = jnp.concatenate([cm_ref[...], cx_ref[...]], axis=0)
    st = jnp.concatenate([sm_ref[...], sx_ref[...]], axis=0)
    p1 = _dot(ct, uc_ref[...])
    p2 = _dot(st, us_ref[...])
    lo_ref[...] = (p1[:ROWS] - p2[:ROWS]).astype(BF16)
    n_in = ROWS + XROWS
    flip = (_iota((ROWS, n_in), 1) == ROWS - _iota((ROWS, n_in), 0)).astype(BF16)
    hi_ref[...] = _dot(flip, (p1 + p2).astype(BF16)).astype(BF16)


def _fourier_ctx_kernel(ct_ref, st_ref, uc_ref, us_ref, o_ref):
    o_ref[...] = (_dot(ct_ref[...], uc_ref[...]) - _dot(st_ref[...], us_ref[...])).astype(BF16)


def _fourier(p3, seq, ctx_len, group_dim):
    bsz, tt, _ = p3.shape
    fw = 512
    halfpos = seq // 2
    n_tiles = halfpos // ROWS

    def tables(rows, period, scale):
        k = (rows[:, None] * jnp.arange(period, dtype=jnp.int32)[None, :]) % period
        ang = k.astype(F32) * (2.0 * jnp.pi / period)
        return (jnp.cos(ang) * scale).astype(BF16), (jnp.sin(ang) * scale).astype(BF16)

    scale = (seq * group_dim) ** -0.5
    cm, sm = tables(jnp.arange(halfpos, dtype=jnp.int32), seq, scale)
    xr = ((jnp.arange(n_tiles, dtype=jnp.int32)[:, None] + 1) * ROWS + jnp.arange(XROWS, dtype=jnp.int32)[None, :])
    cx, sx = tables(xr.reshape(-1), seq, scale)
    lo, hi = pl.pallas_call(
        _fourier_lat_kernel,
        grid=(bsz, n_tiles),
        in_specs=[
            pl.BlockSpec((ROWS, seq), lambda b, i: (i, 0)),
            pl.BlockSpec((XROWS, seq), lambda b, i: (i, 0)),
            pl.BlockSpec((ROWS, seq), lambda b, i: (i, 0)),
            pl.BlockSpec((XROWS, seq), lambda b, i: (i, 0)),
            pl.BlockSpec((None, seq, fw), lambda b, i: (b, 0, COL_FC // fw)),
            pl.BlockSpec((None, seq, fw), lambda b, i: (b, 0, COL_FS // fw)),
        ],
        out_specs=[pl.BlockSpec((None, ROWS, fw), lambda b, i: (b, i, 0)),
                   pl.BlockSpec((None, ROWS, fw), lambda b, i: (b, n_tiles - 1 - i, 0))],
        out_shape=[jax.ShapeDtypeStruct((bsz, halfpos, fw), BF16), jax.ShapeDtypeStruct((bsz, halfpos, fw), BF16)],
        compiler_params=_cparams("parallel", "arbitrary"),
        name="fourier_mix",
    )(cm, cx, sm, sx, p3, p3)

    cc, sc = tables(jnp.arange(ctx_len, dtype=jnp.int32), ctx_len, (ctx_len * group_dim) ** -0.5)
    ctx_out = pl.pallas_call(
        _fourier_ctx_kernel,
        grid=(bsz,),
        in_specs=[
            pl.BlockSpec((ctx_len, ctx_len), lambda b: (0, 0)),
            pl.BlockSpec((ctx_len, ctx_len), lambda b: (0, 0)),
            pl.BlockSpec((None, ctx_len, fw), lambda b: (b, seq // ctx_len, COL_FC // fw)),
            pl.BlockSpec((None, ctx_len, fw), lambda b: (b, seq // ctx_len, COL_FS // fw)),
        ],
        out_specs=pl.BlockSpec((None, ctx_len, fw), lambda b: (b, 0, 0)),
        out_shape=jax.ShapeDtypeStruct((bsz, ctx_len, fw), BF16),
        compiler_params=_cparams("parallel"),
        name="fourier_ctx",
    )(cc, sc, p3, p3)
    return jnp.concatenate([lo, hi, ctx_out], axis=1)


def _merge_kernel(of_ref, ob_ref, z_ref, gt_ref, fm_ref, x_ref, mod_ref, on_ref, nf_ref,
                  wf_ref, wd_ref, wo_ref, wrh_ref, wrl_ref, br_ref, xo_ref, h2_ref, rt_ref, cnt_out_ref, cnt_ref):
    m = mod_ref[0]
    o = of_ref[...].astype(F32) + ob_ref[...].astype(F32)
    on = on_ref[...]
    parts = []
    for h in range(HEADS):
        hs = slice(h * HEAD_DIM, (h + 1) * HEAD_DIM)
        oh = o[:, hs]
        z = z_ref[:, hs].astype(F32)
        y = oh * lax.rsqrt(jnp.mean(oh * oh, axis=-1, keepdims=True) + EPS) * on
        parts.append((y * (z * _sigmoid(z))).astype(BF16))
    od = jnp.concatenate(parts, axis=1)
    pa = _dot(fm_ref[...], wf_ref[...])
    pb = _dot(od, wd_ref[...])
    d = pa.shape[1]
    ga = _sigmoid(gt_ref[:, :d].astype(F32))
    gb = _sigmoid(gt_ref[:, d:].astype(F32))
    y = _dot((ga * pa + gb * pb).astype(BF16), wo_ref[...])
    xn = x_ref[...] + m[2:3] * y
    xo_ref[...] = xn
    h2 = _norm_mod(xn, nf_ref[...], m[3:4], m[4:5])
    h2_ref[...] = h2

    a, b, _ = _split3(h2)
    lg = _dot(a, wrh_ref[...]) + _dot(a, wrl_ref[...]) + _dot(b, wrh_ref[...]) + br_ref[...]
    lane = _iota(lg.shape, 1).astype(F32)
    big = jnp.float32(1 << 20)
    ninf = jnp.float32(-jnp.inf)
    glog = jnp.where(lane < N_GROUPS, lg, ninf)
    gmax = jnp.max(glog, axis=-1, keepdims=True)
    grp = jnp.min(jnp.where(glog == gmax, lane, big), axis=-1, keepdims=True)
    p_grp = 1.0 / jnp.sum(jnp.exp(glog - gmax), axis=-1, keepdims=True)
    lo = N_GROUPS + grp * EXPERTS_PER_GROUP
    el = jnp.where((lane >= lo) & (lane < lo + EXPERTS_PER_GROUP), lg, ninf)
    v1 = jnp.max(el, axis=-1, keepdims=True)
    i1 = jnp.min(jnp.where(el == v1, lane, big), axis=-1, keepdims=True)
    el2 = jnp.where(lane == i1, ninf, el)
    v2 = jnp.max(el2, axis=-1, keepdims=True)
    i2 = jnp.min(jnp.where(el2 == v2, lane, big), axis=-1, keepdims=True)
    e2 = jnp.exp(v2 - v1)
    w1 = p_grp / (1.0 + e2)
    w2 = p_grp * e2 / (1.0 + e2)
    @pl.when(pl.program_id(0) == 0)
    def _():
        cnt_ref[...] = jnp.zeros_like(cnt_ref)

    rows = lg.shape[0]
    onehot = jnp.where((lane == i1) | (lane == i2), 1.0, 0.0).astype(BF16)
    earlier = (_iota((rows, rows), 1) < _iota((rows, rows), 0)).astype(BF16)
    before = _dot(earlier, onehot) + cnt_ref[0:1, :]
    rank1 = jnp.sum(jnp.where(lane == i1, before, 0.0), axis=-1, keepdims=True)
    rank2 = jnp.sum(jnp.where(lane == i2, before, 0.0), axis=-1, keepdims=True)
    cnt_ref[...] = cnt_ref[...] + _dot(jnp.ones((8, rows), BF16), onehot)
    cnt_out_ref[...] = cnt_ref[...]
    rt = jnp.where(lane == 0, i1 - N_GROUPS,
                   jnp.where(lane == 1, i2 - N_GROUPS,
                             jnp.where(lane == 2, w1,
                                       jnp.where(lane == 3, w2,
                                                 jnp.where(lane == 4, rank1, jnp.where(lane == 5, rank2, 0.0))))))
    rt_ref[...] = rt


def _merge(of, ob, p_main, fm, x2, mod, on_t, nf, wf, wd, wo, wr, br, tpb, bsz):
    wr_hi = wr.astype(BF16)
    wr_lo = (wr - wr_hi.astype(F32)).astype(BF16)
    n, d = x2.shape
    tile = lambda w, c=0: pl.BlockSpec((ROWS, w), lambda t: (t, c))
    full = lambda a: pl.BlockSpec(a.shape, lambda t: (0,) * a.ndim)
    return pl.pallas_call(
        _merge_kernel,
        grid=(n // ROWS,),
        in_specs=[
            tile(DW), tile(DW),
            tile(DW, COL_Z // DW),
            tile(2 * d, COL_G // (2 * d)),
            tile(fm.shape[1]),
            tile(d),
            pl.BlockSpec((1, N_MOD, d), _mod_row(tpb, bsz)),
            full(on_t), full(nf), full(wf), full(wd), full(wo), full(wr_hi), full(wr_lo), full(br),
        ],
        out_specs=[tile(d), tile(d), tile(LANES), pl.BlockSpec((8, LANES), lambda t: (0, 0))],
        out_shape=[jax.ShapeDtypeStruct((n, d), F32), jax.ShapeDtypeStruct((n, d), F32),
                   jax.ShapeDtypeStruct((n, LANES), F32), jax.ShapeDtypeStruct((8, LANES), F32)],
        scratch_shapes=[pltpu.VMEM((8, LANES), F32)],
        compiler_params=_cparams("arbitrary"),
        name="merge_route",
    )(of, ob, p_main, p_main, fm, x2, mod, on_t, nf, wf, wd, wo, wr_hi, wr_lo, br)


def _dispatch_kernel(slot_ref, h_ref, xs_in_ref, xs_ref, sem):
    del xs_in_ref

    def copy(r, k):
        return pltpu.make_async_copy(h_ref.at[pl.ds(r, 1)], xs_ref.at[pl.ds(slot_ref[k, r], 1)], sem)

    def start(r8, carry):
        for j in range(8):
            for k in range(TOP_K):
                copy(r8 * 8 + j, k).start()
        return carry

    lax.fori_loop(0, ROWS // 8, start, 0)
    for r in range(ROWS):
        for k in range(TOP_K):
            copy(r, k).wait()


def _dispatch(slots, h2, n_slots):
    n, d = h2.shape
    return pl.pallas_call(
        _dispatch_kernel,
        grid=(n // ROWS,),
        in_specs=[
            pl.BlockSpec((None, TOP_K, ROWS), lambda t: (t, 0, 0), memory_space=pltpu.SMEM),
            pl.BlockSpec((ROWS, d), lambda t: (t, 0)),
            pl.BlockSpec(memory_space=pl.ANY),
        ],
        out_specs=pl.BlockSpec(memory_space=pl.ANY),
        out_shape=jax.ShapeDtypeStruct((n_slots, d), F32),
        scratch_shapes=[pltpu.SemaphoreType.DMA(())],
        input_output_aliases={2: 0},
        compiler_params=_cparams("arbitrary"),
        name="moe_dispatch",
    )(slots, h2, jnp.zeros((n_slots, d), F32))


def _expert_kernel(be_ref, nu_ref, x_ref, wg_ref, wu_ref, wd_ref, y_ref, wgb_ref, wub_ref, wdb_ref):
    i = pl.program_id(0)

    @pl.when((i == 0) | (be_ref[i] != be_ref[jnp.maximum(i - 1, 0)]))
    def _():
        wgb_ref[...] = wg_ref[...].astype(BF16)
        wub_ref[...] = wu_ref[...].astype(BF16)
        wdb_ref[...] = wd_ref[...].astype(BF16)

    @pl.when(i < nu_ref[0])
    def _():
        x = x_ref[...].astype(BF16)
        a = _dot(x, wgb_ref[...])
        u = _dot(x, wub_ref[...])
        hmid = (a * _sigmoid(a) * u).astype(BF16)
        y_ref[...] = _dot(hmid, wdb_ref[...])

    @pl.when(i >= nu_ref[0])
    def _():
        y_ref[...] = jnp.zeros_like(y_ref)


def _experts(blk_e, n_used, xs, wg, wu, wd, layer):
    n_slots, d = xs.shape
    de = wg.shape[-1]
    grid_spec = pltpu.PrefetchScalarGridSpec(
        num_scalar_prefetch=2,
        grid=(n_slots // MOE_BLOCK,),
        in_specs=[
            pl.BlockSpec((MOE_BLOCK, d), lambda i, be, nu: (i, 0)),
            pl.BlockSpec((None, None, d, de), lambda i, be, nu: (layer, be[i], 0, 0)),
            pl.BlockSpec((None, None, d, de), lambda i, be, nu: (layer, be[i], 0, 0)),
            pl.BlockSpec((None, None, de, d), lambda i, be, nu: (layer, be[i], 0, 0)),
        ],
        out_specs=pl.BlockSpec((MOE_BLOCK, d), lambda i, be, nu: (i, 0)),
        scratch_shapes=[pltpu.VMEM((d, de), BF16), pltpu.VMEM((d, de), BF16), pltpu.VMEM((de, d), BF16)],
    )
    return pl.pallas_call(
        _expert_kernel,
        grid_spec=grid_spec,
        out_shape=jax.ShapeDtypeStruct((n_slots, d), F32),
        compiler_params=_cparams("arbitrary"),
        name="moe_experts",
    )(blk_e, n_used, xs, wg, wu, wd)


def _combine_kernel(x_ref, y0_ref, y1_ref, rt_ref, mod_ref, o_ref):
    m = mod_ref[0]
    rt = rt_ref[...]
    y = rt[:, 2:3] * y0_ref[...] + rt[:, 3:4] * y1_ref[...]
    o_ref[...] = x_ref[...] + m[5:6] * y


def _combine(x2, y0, y1, rt, mod, tpb, bsz):
    n, d = x2.shape
    tile = lambda w: pl.BlockSpec((ROWS, w), lambda t: (t, 0))
    return pl.pallas_call(
        _combine_kernel,
        grid=(n // ROWS,),
        in_specs=[tile(d), tile(d), tile(d), tile(LANES), pl.BlockSpec((1, N_MOD, d), _mod_row(tpb, bsz))],
        out_specs=tile(d),
        out_shape=jax.ShapeDtypeStruct((n, d), F32),
        compiler_params=_cparams("parallel"),
        name="moe_combine",
    )(x2, y0, y1, rt, mod)


def _route_tables(rt, counts_f, n):
    counts = counts_f[0, N_GROUPS:N_GROUPS + N_EXPERTS].astype(jnp.int32)
    padded = (counts + MOE_BLOCK - 1) // MOE_BLOCK * MOE_BLOCK
    pad_end = jnp.cumsum(padded)
    pad_start = pad_end - padded
    eid = rt[:, 0:TOP_K].astype(jnp.int32)
    rank = rt[:, 4:4 + TOP_K].astype(jnp.int32)
    onehot = eid[:, :, None] == jnp.arange(N_EXPERTS, dtype=jnp.int32)
    slot_of = jnp.sum(jnp.where(onehot, pad_start, 0), axis=-1) + rank
    n_slots = -(-n * TOP_K // MOE_BLOCK) * MOE_BLOCK + N_EXPERTS * MOE_BLOCK
    blk_start = jnp.arange(n_slots // MOE_BLOCK, dtype=jnp.int32) * MOE_BLOCK
    blk_e = jnp.minimum(jnp.sum(blk_start[:, None] >= pad_end[None, :], axis=-1), N_EXPERTS - 1).astype(jnp.int32)
    n_used = (pad_end[-1] // MOE_BLOCK).astype(jnp.int32).reshape(1)
    return slot_of, blk_e, n_used, n_slots


def _final_kernel(x_ref, w_ref, o_ref):
    x = x_ref[...]
    o_ref[...] = x * lax.rsqrt(jnp.mean(x * x, axis=-1, keepdims=True) + EPS) * w_ref[...]


def _final_norm(x3, w, seq):
    bsz, _, d = x3.shape
    return pl.pallas_call(
        _final_kernel,
        grid=(bsz, seq // ROWS),
        in_specs=[pl.BlockSpec((None, ROWS, d), lambda b, i: (b, i, 0)), pl.BlockSpec((1, d), lambda b, i: (0, 0))],
        out_specs=pl.BlockSpec((None, ROWS, d), lambda b, i: (b, i, 0)),
        out_shape=jax.ShapeDtypeStruct((bsz, seq, d), F32),
        compiler_params=_cparams("parallel", "parallel"),
        name="final_norm",
    )(x3, w)


def _channel_dft(group_dim):
    k = (jnp.arange(group_dim, dtype=jnp.int32)[:, None] * jnp.arange(group_dim, dtype=jnp.int32)[None, :]) % group_dim
    ang = k.astype(F32) * (2.0 * jnp.pi / group_dim)
    eye = jnp.eye(FOURIER_GROUPS, dtype=F32)
    return jnp.kron(eye, jnp.cos(ang)), jnp.kron(eye, jnp.sin(ang))


def kernel(x, c, ctx, c_ctx, w_mod, b_mod, norm_mix, norm_ffn, w_in, conv_w, a_log, dt_bias, out_norm,
           w_fourier, w_delta, w_out, w_route_group, b_route_group, w_route_expert, b_route_expert,
           w_gate, w_up, w_down, final_norm):
    bsz, seq, d = x.shape
    ctx_len = ctx.shape[1]
    depth = w_mod.shape[0]
    assert d == HEADS * HEAD_DIM and ctx_len == ROWS and seq % ROWS == 0 and bsz < 16
    tt = seq + ctx_len
    tpb = tt // ROWS
    n = bsz * tt
    fw = w_fourier.shape[1]
    group_dim = fw // FOURIER_GROUPS

    cvec = jnp.zeros((16, d), F32).at[:bsz].set(c).at[bsz].set(c_ctx)
    mod_all = _modulation(cvec, w_mod, b_mod).reshape(depth, 16, N_MOD, d)
    cc, sc = _channel_dft(group_dim)

    o_a = 3 * DW
    o_b = o_a + 2 * HEADS
    o_z = o_b + 2 * HEADS
    o_f = o_z + DW
    o_g = o_f + fw

    xcur = jnp.concatenate([x, ctx], axis=1).reshape(n, d)
    for i in range(depth):
        wi = w_in[i]
        wfold_c = jnp.dot(wi[:, o_f:o_g], cc, precision=lax.Precision.HIGHEST)
        wfold_s = jnp.dot(wi[:, o_f:o_g], sc, precision=lax.Precision.HIGHEST)
        w_main = jnp.concatenate([wi[:, :o_a], wi[:, o_z:o_f], wi[:, o_g:], wfold_c, wfold_s], axis=1).astype(BF16)
        w_ab = jnp.concatenate([wi[:, o_a:o_z], jnp.zeros((d, LANES - 4 * HEADS), F32)], axis=1).astype(BF16)
        gate_params = jnp.zeros((8, LANES), F32)
        gate_params = gate_params.at[0, :2 * HEADS].set(a_log[i].reshape(-1)).at[1, :2 * HEADS].set(dt_bias[i].reshape(-1))
        mod = mod_all[i]

        p_main, ab = _inproj(xcur, mod, norm_mix[i].reshape(1, d), w_main, w_ab, tpb, bsz)
        u, w, kt, qd, qkd, gs = _prep(p_main, ab, conv_w[i], gate_params, tpb)
        of, ob = _scan(u, w, kt, qd, qkd, gs, bsz, seq // ROWS, ctx_len // ROWS)
        fm = _fourier(p_main.reshape(bsz, tt, MAIN_COLS), seq, ctx_len, group_dim).reshape(n, fw)

        wr = jnp.zeros((d, LANES), F32).at[:, :N_GROUPS].set(w_route_group[i])
        wr = wr.at[:, N_GROUPS:N_GROUPS + N_EXPERTS].set(w_route_expert[i])
        br = jnp.zeros((1, LANES), F32).at[0, :N_GROUPS].set(b_route_group[i])
        br = br.at[0, N_GROUPS:N_GROUPS + N_EXPERTS].set(b_route_expert[i])
        xcur, h2, rt, counts = _merge(of, ob, p_main, fm, xcur, mod, out_norm[i].reshape(1, HEAD_DIM),
                                      norm_ffn[i].reshape(1, d), w_fourier[i].astype(BF16),
                                      w_delta[i].astype(BF16), w_out[i].astype(BF16), wr, br, tpb, bsz)

        slot_of, blk_e, n_used, n_slots = _route_tables(rt, counts, n)
        slots = jnp.swapaxes(slot_of.reshape(n // ROWS, ROWS, TOP_K), 1, 2)
        xs = _dispatch(slots, h2, n_slots)
        ys = _experts(blk_e, n_used, xs, w_gate, w_up, w_down, i)
        xcur = _combine(xcur, ys[slot_of[:, 0]], ys[slot_of[:, 1]], rt, mod, tpb, bsz)

    return _final_norm(xcur.reshape(bsz, tt, d), final_norm.reshape(1, d), seq)
```

```python
import functools

import jax
import jax.numpy as jnp
from jax import lax
from jax.experimental import pallas as pl
from jax.experimental.pallas import tpu as pltpu

F32 = jnp.float32
BF16 = jnp.bfloat16

EPS = 1e-6
ROWS = 256
CHUNK = 64
HEADS = 8
HEAD_DIM = 128
PAIR = 2 * HEAD_DIM
N_MOD = 6
N_GROUPS = 4
EXPERTS_PER_GROUP = 8
N_EXPERTS = N_GROUPS * EXPERTS_PER_GROUP
TOP_K = 2
MOE_BLOCK = 256
FOURIER_GROUPS = 4
LANES = 128
VMEM_LIMIT = 56 * 1024 * 1024


def _cparams(*sem):
    return pltpu.CompilerParams(dimension_semantics=sem, vmem_limit_bytes=VMEM_LIMIT)


def _sigmoid(x):
    return 1.0 / (1.0 + jnp.exp(-x))


def _dot(a, b):
    return jnp.dot(a, b, preferred_element_type=F32)


def _dot_nt(a, b):
    return lax.dot_general(a, b, (((1,), (1,)), ((), ())), preferred_element_type=F32)


def _dot_tn(a, b):
    return lax.dot_general(a, b, (((0,), (0,)), ((), ())), preferred_element_type=F32)


def _split3(x):
    h1 = x.astype(BF16)
    r1 = x - h1.astype(F32)
    h2 = r1.astype(BF16)
    r2 = r1 - h2.astype(F32)
    return h1, h2, r2.astype(BF16)


def _dot_x01(x, m01):
    a, b, c = _split3(x)
    return _dot(a, m01) + _dot(b, m01) + _dot(c, m01)


def _dot_01x(m01, x):
    a, b, c = _split3(x)
    return _dot(m01, a) + _dot(m01, b) + _dot(m01, c)


def _iota(shape, dim):
    return lax.broadcasted_iota(jnp.int32, shape, dim)


def _mod_kernel(c_ref, w_ref, b_ref, o_ref):
    c = c_ref[...]
    s = c * _sigmoid(c)
    w = w_ref[0]
    a, b, _ = _split3(s)
    wa, wb, _ = _split3(w)
    o_ref[0] = _dot(a, wa) + _dot(a, wb) + _dot(b, wa) + b_ref[0]


def _modulation(cvec, w_mod, b_mod):
    depth, d, n = w_mod.shape
    tn = 1536
    return pl.pallas_call(
        _mod_kernel,
        grid=(depth, n // tn),
        in_specs=[
            pl.BlockSpec((16, d), lambda i, j: (0, 0)),
            pl.BlockSpec((1, d, tn), lambda i, j: (i, 0, j)),
            pl.BlockSpec((1, 1, tn), lambda i, j: (i, 0, j)),
        ],
        out_specs=pl.BlockSpec((1, 16, tn), lambda i, j: (i, 0, j)),
        out_shape=jax.ShapeDtypeStruct((depth, 16, n), F32),
        compiler_params=_cparams("arbitrary", "arbitrary"),
        name="modulation",
    )(cvec, w_mod, b_mod.reshape(depth, 1, n))


COL_QKV = 0
COL_Z = 3 * HEADS * HEAD_DIM
COL_G = COL_Z + HEADS * HEAD_DIM
COL_FC = COL_G + 2 * 1024
COL_FS = COL_FC + 512
MAIN_COLS = COL_FS + 512
COL_TILE = 512


def _norm_mod(x, nw, shift, scale):
    ms = jnp.mean(x * x, axis=-1, keepdims=True)
    return (x * lax.rsqrt(ms + EPS) * nw) * (1.0 + scale) + shift


def _inproj_kernel(x_ref, mod_ref, nw_ref, w_ref, wab_ref, p_ref, ab_ref):
    m = mod_ref[0]
    h = _norm_mod(x_ref[...], nw_ref[...], m[0:1], m[1:2]).astype(BF16)
    for c in range(MAIN_COLS // COL_TILE):
        cs = slice(c * COL_TILE, (c + 1) * COL_TILE)
        p_ref[:, cs] = _dot(h, w_ref[:, cs]).astype(BF16)
    ab_ref[...] = _dot(h, wab_ref[...])


def _mod_row(tpb, bsz):
    return lambda t: (jnp.where(t % tpb == tpb - 1, bsz, t // tpb), 0, 0)


def _inproj(x2, mod, nw, w_main, w_ab, tpb, bsz):
    n, d = x2.shape
    return pl.pallas_call(
        _inproj_kernel,
        grid=(n // ROWS,),
        in_specs=[
            pl.BlockSpec((ROWS, d), lambda t: (t, 0)),
            pl.BlockSpec((1, N_MOD, d), _mod_row(tpb, bsz)),
            pl.BlockSpec((1, d), lambda t: (0, 0)),
            pl.BlockSpec((d, MAIN_COLS), lambda t: (0, 0)),
            pl.BlockSpec((d, LANES), lambda t: (0, 0)),
        ],
        out_specs=[
            pl.BlockSpec((ROWS, MAIN_COLS), lambda t: (t, 0)),
            pl.BlockSpec((ROWS, LANES), lambda t: (t, 0)),
        ],
        out_shape=[
            jax.ShapeDtypeStruct((n, MAIN_COLS), BF16),
            jax.ShapeDtypeStruct((n, LANES), F32),
        ],
        compiler_params=_cparams("parallel"),
        name="inproj",
    )(x2, mod, nw, w_main, w_ab)


DW = HEADS * HEAD_DIM
DCOLS = HEADS * CHUNK
SOLVE_CHUNKS = 4


def _blockdiag_rows(x, width):
    lane_blk = _iota(x.shape, 1) // width
    zero = jnp.zeros_like(x)
    return jnp.concatenate([jnp.where(lane_blk == u, x, zero) for u in range(x.shape[1] // width)], axis=0)


def _prep_kernel(tpb, qkv_ref, ab_ref, cw_ref, gp_ref,
                 u_ref, w_ref, kt_ref, qd_ref, qkd_ref, gs_ref,
                 kn_ref, qs_ref, kb_ref, rw_ref, ru_ref, dm_ref):
    t = pl.program_id(0)
    is_ctx = (t % tpb) == (tpb - 1)
    r = _iota((ROWS, 1), 0)
    rp = jnp.where(is_ctx, r, r % CHUNK)
    has_prev = rp != 0
    has_next = rp != jnp.where(is_ctx, ROWS - 1, CHUNK - 1)

    ab = ab_ref[...]
    gp = gp_ref[...]
    xg = ab + gp[1:2]
    softplus = jnp.maximum(xg, 0.0) + jnp.log(1.0 + jnp.exp(-jnp.abs(xg)))
    g = -jnp.exp(gp[0:1]) * softplus
    sig = _sigmoid(ab)

    ri = _iota((ROWS, ROWS), 0)
    ci = _iota((ROWS, ROWS), 1)
    same = (ri // CHUNK) == (ci // CHUNK)
    tri_lo = (same & (ci <= ri)).astype(BF16)
    tri_up = (same & (ci >= ri)).astype(BF16)
    ones_bd = same.astype(BF16)
    lane = _iota((ROWS, LANES), 1)
    gc = jnp.where(lane < HEADS, _dot_01x(tri_lo, g), _dot_01x(tri_up, g))
    tot = _dot_01x(ones_bd, g)
    eg = jnp.exp(gc)
    ekt = jnp.exp(tot - gc)
    gs_ref[...] = jnp.exp(tot)

    er = _iota((LANES, DCOLS), 0)
    ec = _iota((LANES, DCOLS), 1)
    di = _iota((ROWS, DCOLS), 0) % CHUNK
    dj = _iota((ROWS, DCOLS), 1) % CHUNK
    for d in range(2):
        expand = (er - d * HEADS == ec // CHUNK).astype(BF16)
        gcol = _dot_x01(gc, expand)
        grow = _dot_01x(ones_bd, jnp.where(di == dj, gcol, 0.0))
        keep = (dj <= di) if d == 0 else (dj >= di)
        dm_ref[:, d * DCOLS:(d + 1) * DCOLS] = jnp.where(keep, jnp.exp(jnp.where(keep, gcol - grow, 0.0)), 0.0)

    cw = cw_ref[...]

    def conv_silu(col):
        cs = slice(col * HEAD_DIM, (col + 1) * HEAD_DIM)
        x = qkv_ref[:, cs].astype(F32)
        prev = jnp.where(has_prev, pltpu.roll(x, 1, 0), 0.0)
        nxt = jnp.where(has_next, pltpu.roll(x, ROWS - 1, 0), 0.0)
        y = cw[1:2, cs] * x + cw[0:1, cs] * prev + cw[2:3, cs] * nxt
        return y * _sigmoid(y)

    for h in range(HEADS):
        hs = slice(h * HEAD_DIM, (h + 1) * HEAD_DIM)
        q = conv_silu(h)
        k = conv_silu(HEADS + h)
        v = conv_silu(2 * HEADS + h)
        qn = q * lax.rsqrt(jnp.sum(q * q, axis=-1, keepdims=True) + EPS) * (HEAD_DIM ** -0.5)
        kn = k * lax.rsqrt(jnp.sum(k * k, axis=-1, keepdims=True) + EPS)
        kn_ref[:, hs] = kn.astype(BF16)
        qs_ref[:, hs] = qn.astype(BF16)
        for d in range(2):
            c = d * HEADS + h
            beta = sig[:, 2 * HEADS + c:2 * HEADS + c + 1]
            egc = eg[:, c:c + 1]
            kb = kn * beta
            kb_ref[d, :, hs] = kb.astype(BF16)
            rw_ref[d, :, hs] = (kb * egc).astype(BF16)
            ru_ref[d, :, hs] = (v * beta).astype(BF16)
            kt_ref[d, :, hs] = (kn * ekt[:, c:c + 1]).astype(BF16)
            qd_ref[d, :, hs] = (qn * egc).astype(BF16)

    i64 = _iota((CHUNK, PAIR), 0)
    j64 = _iota((CHUNK, PAIR), 1) % CHUNK
    eye4 = (i64 == j64).astype(F32)
    diag2 = _iota((CHUNK, 2 * CHUNK), 0) == _iota((CHUNK, 2 * CHUNK), 1) % CHUNK
    BASE = 4
    blk = {}
    s = BASE
    while s <= CHUNK:
        blk[s] = (i64 // s) == (j64 // s)
        s *= 2

    def solve_chunks(it, carry):
        chains = []
        for cj in range(SOLVE_CHUNKS):
            rows = pl.ds(pl.multiple_of((it * SOLVE_CHUNKS + cj) * CHUNK, CHUNK), CHUNK)
            chains += [(rows, d, g) for d in range(2) for g in range(HEADS // 4)]

        def bd(x):
            return _blockdiag_rows(x.astype(BF16), CHUNK)

        ms = []
        for rows, d, g in chains:
            neg_l = []
            for p in range(2):
                pp = 2 * g + p
                ls = slice(pp * PAIR, (pp + 1) * PAIR)
                lhs = jnp.concatenate([kb_ref[d, rows, ls], qs_ref[rows, ls]], axis=0)
                gram = _dot_nt(lhs, _blockdiag_rows(kn_ref[rows, ls], HEAD_DIM))
                dm = dm_ref[rows, d * DCOLS + pp * 2 * CHUNK:d * DCOLS + (pp + 1) * 2 * CHUNK]
                neg_l.append(jnp.where(diag2, 0.0, -gram[:CHUNK] * dm))
                qkd_ref[d, rows, pp * 2 * CHUNK:(pp + 1) * 2 * CHUNK] = (gram[CHUNK:] * dm).astype(BF16)
            ms.append(jnp.concatenate(neg_l, axis=1))
        m4 = [jnp.where(blk[BASE], m, 0.0) for m in ms]
        m4sq = [_dot(x.astype(BF16), bd(x)) for x in m4]
        t_inv = [(eye4 + x) + _dot((eye4 + x).astype(BF16), bd(y)) for x, y in zip(m4, m4sq)]
        s = BASE
        while s < CHUNK:
            offd = blk[2 * s] & ~blk[s]
            ct = [_dot(jnp.where(offd, m, 0.0).astype(BF16), bd(t)) for m, t in zip(ms, t_inv)]
            t_inv = [t + _dot(t.astype(BF16), bd(c)) for t, c in zip(t_inv, ct)]
            s *= 2
        uws = []
        for (rows, d, g), t in zip(chains, t_inv):
            rhs = jnp.concatenate(
                [jnp.concatenate([ru_ref[d, rows, (4 * g + u) * HEAD_DIM:(4 * g + u + 1) * HEAD_DIM],
                                  rw_ref[d, rows, (4 * g + u) * HEAD_DIM:(4 * g + u + 1) * HEAD_DIM]], axis=1)
                 for u in range(4)], axis=0)
            uws.append(_dot(bd(t), rhs))
        for (rows, d, g), uw in zip(chains, uws):
            for u in range(4):
                hs = slice((4 * g + u) * HEAD_DIM, (4 * g + u + 1) * HEAD_DIM)
                u_ref[d, rows, hs] = uw[u * CHUNK:(u + 1) * CHUNK, :HEAD_DIM].astype(BF16)
                w_ref[d, rows, hs] = uw[u * CHUNK:(u + 1) * CHUNK, HEAD_DIM:].astype(BF16)
        return carry

    lax.fori_loop(0, ROWS // (CHUNK * SOLVE_CHUNKS), solve_chunks, 0)


def _prep(p_main, ab, conv_w, gate_params, tpb):
    n = p_main.shape[0]
    tile = lambda w: pl.BlockSpec((ROWS, w), lambda t: (t, 0))
    tile2 = lambda w: pl.BlockSpec((2, ROWS, w), lambda t: (0, t, 0))
    perdir = jax.ShapeDtypeStruct((2, n, DW), BF16)
    return pl.pallas_call(
        functools.partial(_prep_kernel, tpb),
        grid=(n // ROWS,),
        in_specs=[
            tile(3 * DW),
            tile(LANES),
            pl.BlockSpec((3, 3 * DW), lambda t: (0, 0)),
            pl.BlockSpec((8, LANES), lambda t: (0, 0)),
        ],
        out_specs=[tile2(DW), tile2(DW), tile2(DW), tile2(DW), tile2(DCOLS), tile(LANES)],
        out_shape=[perdir, perdir, perdir, perdir, jax.ShapeDtypeStruct((2, n, DCOLS), BF16),
                   jax.ShapeDtypeStruct((n, LANES), F32)],
        scratch_shapes=[pltpu.VMEM((ROWS, DW), BF16), pltpu.VMEM((ROWS, DW), BF16),
                        pltpu.VMEM((2, ROWS, DW), BF16), pltpu.VMEM((2, ROWS, DW), BF16),
                        pltpu.VMEM((2, ROWS, DW), BF16), pltpu.VMEM((ROWS, 2 * DCOLS), F32)],
        compiler_params=_cparams("parallel"),
        name="delta_prep",
    )(p_main, ab, conv_w, gate_params)


def _scan_kernel(uf, wf, ktf, qdf, qkf, gsf, ub, wb, ktb, qdb, qkb, gsb, of_ref, ob_ref, sf_ref, sb_ref):
    @pl.when(pl.program_id(1) == 0)
    def _():
        sf_ref[...] = jnp.zeros_like(sf_ref)
        sb_ref[...] = jnp.zeros_like(sb_ref)

    dirs = ((uf, wf, ktf, qdf, qkf, gsf, of_ref, sf_ref), (ub, wb, ktb, qdb, qkb, gsb, ob_ref, sb_ref))
    lane_p = _iota((8, PAIR), 1)
    bd_mask = (_iota((PAIR, PAIR), 0) // HEAD_DIM) == (_iota((PAIR, PAIR), 1) // HEAD_DIM)
    n_chunks = ROWS // CHUNK
    chains = [(d, pp) for d in range(2) for pp in range(HEADS // 2)]
    for step in range(n_chunks):
        def rows_of(d):
            ck = step if d == 0 else n_chunks - 1 - step
            return slice(ck * CHUNK, (ck + 1) * CHUNK)

        s_old, wq_s, v_new_b = [], [], []
        for d, pp in chains:
            u_ref, w_ref, kt_ref, qd_ref, qkd_ref, gs_ref, o_ref, s_ref = dirs[d]
            rows, ls = rows_of(d), slice(pp * PAIR, (pp + 1) * PAIR)
            s_old.append(s_ref[pp])
            wq_s.append(_dot(jnp.concatenate([w_ref[rows, ls], qd_ref[rows, ls]], axis=0), s_old[-1].astype(BF16)))
        for i, (d, pp) in enumerate(chains):
            u_ref = dirs[d][0]
            rows, ls = rows_of(d), slice(pp * PAIR, (pp + 1) * PAIR)
            v_new_b.append((u_ref[rows, ls].astype(F32) - wq_s[i][:CHUNK]).astype(BF16))
        for i, (d, pp) in enumerate(chains):
            u_ref, w_ref, kt_ref, qd_ref, qkd_ref, gs_ref, o_ref, s_ref = dirs[d]
            rows, ls = rows_of(d), slice(pp * PAIR, (pp + 1) * PAIR)
            qk = qkd_ref[rows, pp * 2 * CHUNK:(pp + 1) * 2 * CHUNK]
            o_ref[rows, ls] = (wq_s[i][CHUNK:] + _dot(qk, _blockdiag_rows(v_new_b[i], HEAD_DIM))).astype(BF16)
        for i, (d, pp) in enumerate(chains):
            u_ref, w_ref, kt_ref, qd_ref, qkd_ref, gs_ref, o_ref, s_ref = dirs[d]
            rows, ls = rows_of(d), slice(pp * PAIR, (pp + 1) * PAIR)
            gs = gs_ref[rows.start:rows.start + 8, :]
            c0 = d * HEADS + 2 * pp
            decay = jnp.where(lane_p < HEAD_DIM, gs[:, c0:c0 + 1], gs[:, c0 + 1:c0 + 2])
            upd = _dot_tn(kt_ref[rows, ls], v_new_b[i])
            s_dec = (s_old[i].reshape(PAIR // 8, 8, PAIR) * decay[None]).reshape(PAIR, PAIR)
            s_ref[pp] = s_dec + jnp.where(bd_mask, upd, 0.0)


def _scan(u, w, kt, qd, qkd, gs, bsz, n_lat_blocks, n_ctx_blocks):
    nb = n_lat_blocks + n_ctx_blocks
    fwd = lambda j: jnp.where(j < n_ctx_blocks, n_lat_blocks + j, j - n_ctx_blocks)
    bwd = lambda j: nb - 1 - j
    r5 = lambda a: a.reshape(2, bsz, nb, ROWS, a.shape[-1])

    def perdir(d, cmap, w):
        return pl.BlockSpec((None, None, None, ROWS, w), lambda b, j: (d, b, cmap(j), 0, 0))

    def shared(cmap, w):
        return pl.BlockSpec((None, None, ROWS, w), lambda b, j: (b, cmap(j), 0, 0))

    def specs(d, cmap):
        return [perdir(d, cmap, DW), perdir(d, cmap, DW), perdir(d, cmap, DW), perdir(d, cmap, DW),
                perdir(d, cmap, DCOLS), shared(cmap, LANES)]

    args = (r5(u), r5(w), r5(kt), r5(qd), r5(qkd), gs.reshape(bsz, nb, ROWS, LANES))
    o_shape = jax.ShapeDtypeStruct((bsz, nb, ROWS, DW), BF16)
    of, ob = pl.pallas_call(
        _scan_kernel,
        grid=(bsz, nb),
        in_specs=specs(0, fwd) + specs(1, bwd),
        out_specs=[shared(fwd, DW), shared(bwd, DW)],
        out_shape=[o_shape, o_shape],
        scratch_shapes=[pltpu.VMEM((HEADS // 2, PAIR, PAIR), F32), pltpu.VMEM((HEADS // 2, PAIR, PAIR), F32)],
        compiler_params=_cparams("parallel", "arbitrary"),
        name="delta_scan",
    )(*args, *args)
    return of.reshape(-1, DW), ob.reshape(-1, DW)


XROWS = 16


def _fourier_lat_kernel(cm_ref, cx_ref, sm_ref, sx_ref, uc_ref, us_ref, lo_ref, hi_ref):
    ct = jnp.concatenate([cm_ref[...], cx_ref[...]], axis=0)
    st = jnp.concatenate([sm_ref[...], sx_ref[...]], axis=0)
    p1 = _dot(ct, uc_ref[...])
    p2 = _dot(st, us_ref[...])
    lo_ref[...] = (p1[:ROWS] - p2[:ROWS]).astype(BF16)
    n_in = ROWS + XROWS
    flip = (_iota((ROWS, n_in), 1) == ROWS - _iota((ROWS, n_in), 0)).astype(BF16)
    hi_ref[...] = _dot(flip, (p1 + p2).astype(BF16)).astype(BF16)


def _fourier_ctx_kernel(ct_ref, st_ref, uc_ref, us_ref, o_ref):
    o_ref[...] = (_dot(ct_ref[...], uc_ref[...]) - _dot(st_ref[...], us_ref[...])).astype(BF16)


def _fourier(p3, seq, ctx_len, group_dim):
    bsz, tt, _ = p3.shape
    fw = 512
    halfpos = seq // 2
    n_tiles = halfpos // ROWS

    def tables(rows, period, scale):
        k = (rows[:, None] * jnp.arange(period, dtype=jnp.int32)[None, :]) % period
        ang = k.astype(F32) * (2.0 * jnp.pi / period)
        return (jnp.cos(ang) * scale).astype(BF16), (jnp.sin(ang) * scale).astype(BF16)

    scale = (seq * group_dim) ** -0.5
    cm, sm = tables(jnp.arange(halfpos, dtype=jnp.int32), seq, scale)
    xr = ((jnp.arange(n_tiles, dtype=jnp.int32)[:, None] + 1) * ROWS + jnp.arange(XROWS, dtype=jnp.int32)[None, :])
    cx, sx = tables(xr.reshape(-1), seq, scale)
    lo, hi = pl.pallas_call(
        _fourier_lat_kernel,
        grid=(bsz, n_tiles),
        in_specs=[
            pl.BlockSpec((ROWS, seq), lambda b, i: (i, 0)),
            pl.BlockSpec((XROWS, seq), lambda b, i: (i, 0)),
            pl.BlockSpec((ROWS, seq), lambda b, i: (i, 0)),
            pl.BlockSpec((XROWS, seq), lambda b, i: (i, 0)),
            pl.BlockSpec((None, seq, fw), lambda b, i: (b, 0, COL_FC // fw)),
            pl.BlockSpec((None, seq, fw), lambda b, i: (b, 0, COL_FS // fw)),
        ],
        out_specs=[pl.BlockSpec((None, ROWS, fw), lambda b, i: (b, i, 0)),
                   pl.BlockSpec((None, ROWS, fw), lambda b, i: (b, n_tiles - 1 - i, 0))],
        out_shape=[jax.ShapeDtypeStruct((bsz, halfpos, fw), BF16), jax.ShapeDtypeStruct((bsz, halfpos, fw), BF16)],
        compiler_params=_cparams("parallel", "arbitrary"),
        name="fourier_mix",
    )(cm, cx, sm, sx, p3, p3)

    cc, sc = tables(jnp.arange(ctx_len, dtype=jnp.int32), ctx_len, (ctx_len * group_dim) ** -0.5)
    ctx_out = pl.pallas_call(
        _fourier_ctx_kernel,
        grid=(bsz,),
        in_specs=[
            pl.BlockSpec((ctx_len, ctx_len), lambda b: (0, 0)),
            pl.BlockSpec((ctx_len, ctx_len), lambda b: (0, 0)),
            pl.BlockSpec((None, ctx_len, fw), lambda b: (b, seq // ctx_len, COL_FC // fw)),
            pl.BlockSpec((None, ctx_len, fw), lambda b: (b, seq // ctx_len, COL_FS // fw)),
        ],
        out_specs=pl.BlockSpec((None, ctx_len, fw), lambda b: (b, 0, 0)),
        out_shape=jax.ShapeDtypeStruct((bsz, ctx_len, fw), BF16),
        compiler_params=_cparams("parallel"),
        name="fourier_ctx",
    )(cc, sc, p3, p3)
    return jnp.concatenate([lo, hi, ctx_out], axis=1)


def _merge_kernel(of_ref, ob_ref, z_ref, gt_ref, fm_ref, x_ref, mod_ref, on_ref, nf_ref,
                  wf_ref, wd_ref, wo_ref, wrh_ref, wrl_ref, br_ref, xo_ref, h2_ref, rt_ref, cnt_out_ref, cnt_ref):
    m = mod_ref[0]
    o = of_ref[...].astype(F32) + ob_ref[...].astype(F32)
    on = on_ref[...]
    parts = []
    for h in range(HEADS):
        hs = slice(h * HEAD_DIM, (h + 1) * HEAD_DIM)
        oh = o[:, hs]
        z = z_ref[:, hs].astype(F32)
        y = oh * lax.rsqrt(jnp.mean(oh * oh, axis=-1, keepdims=True) + EPS) * on
        parts.append((y * (z * _sigmoid(z))).astype(BF16))
    od = jnp.concatenate(parts, axis=1)
    pa = _dot(fm_ref[...], wf_ref[...])
    pb = _dot(od, wd_ref[...])
    d = pa.shape[1]
    ga = _sigmoid(gt_ref[:, :d].astype(F32))
    gb = _sigmoid(gt_ref[:, d:].astype(F32))
    y = _dot((ga * pa + gb * pb).astype(BF16), wo_ref[...])
    xn = x_ref[...] + m[2:3] * y
    xo_ref[...] = xn
    h2 = _norm_mod(xn, nf_ref[...], m[3:4], m[4:5])
    h2_ref[...] = h2

    a, b, _ = _split3(h2)
    lg = _dot(a, wrh_ref[...]) + _dot(a, wrl_ref[...]) + _dot(b, wrh_ref[...]) + br_ref[...]
    lane = _iota(lg.shape, 1).astype(F32)
    big = jnp.float32(1 << 20)
    ninf = jnp.float32(-jnp.inf)
    glog = jnp.where(lane < N_GROUPS, lg, ninf)
    gmax = jnp.max(glog, axis=-1, keepdims=True)
    grp = jnp.min(jnp.where(glog == gmax, lane, big), axis=-1, keepdims=True)
    p_grp = 1.0 / jnp.sum(jnp.exp(glog - gmax), axis=-1, keepdims=True)
    lo = N_GROUPS + grp * EXPERTS_PER_GROUP
    el = jnp.where((lane >= lo) & (lane < lo + EXPERTS_PER_GROUP), lg, ninf)
    v1 = jnp.max(el, axis=-1, keepdims=True)
    i1 = jnp.min(jnp.where(el == v1, lane, big), axis=-1, keepdims=True)
    el2 = jnp.where(lane == i1, ninf, el)
    v2 = jnp.max(el2, axis=-1, keepdims=True)
    i2 = jnp.min(jnp.where(el2 == v2, lane, big), axis=-1, keepdims=True)
    e2 = jnp.exp(v2 - v1)
    w1 = p_grp / (1.0 + e2)
    w2 = p_grp * e2 / (1.0 + e2)
    @pl.when(pl.program_id(0) == 0)
    def _():
        cnt_ref[...] = jnp.zeros_like(cnt_ref)

    rows = lg.shape[0]
    onehot = jnp.where((lane == i1) | (lane == i2), 1.0, 0.0).astype(BF16)
    earlier = (_iota((rows, rows), 1) < _iota((rows, rows), 0)).astype(BF16)
    before = _dot(earlier, onehot) + cnt_ref[0:1, :]
    rank1 = jnp.sum(jnp.where(lane == i1, before, 0.0), axis=-1, keepdims=True)
    rank2 = jnp.sum(jnp.where(lane == i2, before, 0.0), axis=-1, keepdims=True)
    cnt_ref[...] = cnt_ref[...] + _dot(jnp.ones((8, rows), BF16), onehot)
    cnt_out_ref[...] = cnt_ref[...]
    rt = jnp.where(lane == 0, i1 - N_GROUPS,
                   jnp.where(lane == 1, i2 - N_GROUPS,
                             jnp.where(lane == 2, w1,
                                       jnp.where(lane == 3, w2,
                                                 jnp.where(lane == 4, rank1, jnp.where(lane == 5, rank2, 0.0))))))
    rt_ref[...] = rt


def _merge(of, ob, p_main, fm, x2, mod, on_t, nf, wf, wd, wo, wr, br, tpb, bsz):
    wr_hi = wr.astype(BF16)
    wr_lo = (wr - wr_hi.astype(F32)).astype(BF16)
    n, d = x2.shape
    tile = lambda w, c=0: pl.BlockSpec((ROWS, w), lambda t: (t, c))
    full = lambda a: pl.BlockSpec(a.shape, lambda t: (0,) * a.ndim)
    return pl.pallas_call(
        _merge_kernel,
        grid=(n // ROWS,),
        in_specs=[
            tile(DW), tile(DW),
            tile(DW, COL_Z // DW),
            tile(2 * d, COL_G // (2 * d)),
            tile(fm.shape[1]),
            tile(d),
            pl.BlockSpec((1, N_MOD, d), _mod_row(tpb, bsz)),
            full(on_t), full(nf), full(wf), full(wd), full(wo), full(wr_hi), full(wr_lo), full(br),
        ],
        out_specs=[tile(d), tile(d), tile(LANES), pl.BlockSpec((8, LANES), lambda t: (0, 0))],
        out_shape=[jax.ShapeDtypeStruct((n, d), F32), jax.ShapeDtypeStruct((n, d), F32),
                   jax.ShapeDtypeStruct((n, LANES), F32), jax.ShapeDtypeStruct((8, LANES), F32)],
        scratch_shapes=[pltpu.VMEM((8, LANES), F32)],
        compiler_params=_cparams("arbitrary"),
        name="merge_route",
    )(of, ob, p_main, p_main, fm, x2, mod, on_t, nf, wf, wd, wo, wr_hi, wr_lo, br)


def _dispatch_kernel(slot_ref, h_ref, xs_in_ref, xs_ref, sem):
    del xs_in_ref

    def copy(r, k):
        return pltpu.make_async_copy(h_ref.at[pl.ds(r, 1)], xs_ref.at[pl.ds(slot_ref[k, r], 1)], sem)

    def start(r8, carry):
        for j in range(8):
            for k in range(TOP_K):
                copy(r8 * 8 + j, k).start()
        return carry

    lax.fori_loop(0, ROWS // 8, start, 0)
    for r in range(ROWS):
        for k in range(TOP_K):
            copy(r, k).wait()


def _dispatch(slots, h2, n_slots):
    n, d = h2.shape
    return pl.pallas_call(
        _dispatch_kernel,
        grid=(n // ROWS,),
        in_specs=[
            pl.BlockSpec((None, TOP_K, ROWS), lambda t: (t, 0, 0), memory_space=pltpu.SMEM),
            pl.BlockSpec((ROWS, d), lambda t: (t, 0)),
            pl.BlockSpec(memory_space=pl.ANY),
        ],
        out_specs=pl.BlockSpec(memory_space=pl.ANY),
        out_shape=jax.ShapeDtypeStruct((n_slots, d), F32),
        scratch_shapes=[pltpu.SemaphoreType.DMA(())],
        input_output_aliases={2: 0},
        compiler_params=_cparams("arbitrary"),
        name="moe_dispatch",
    )(slots, h2, jnp.zeros((n_slots, d), F32))


def _expert_kernel(be_ref, nu_ref, x_ref, wg_ref, wu_ref, wd_ref, y_ref, wgb_ref, wub_ref, wdb_ref):
    i = pl.program_id(0)

    @pl.when((i == 0) | (be_ref[i] != be_ref[jnp.maximum(i - 1, 0)]))
    def _():
        wgb_ref[...] = wg_ref[...].astype(BF16)
        wub_ref[...] = wu_ref[...].astype(BF16)
        wdb_ref[...] = wd_ref[...].astype(BF16)

    @pl.when(i < nu_ref[0])
    def _():
        x = x_ref[...].astype(BF16)
        a = _dot(x, wgb_ref[...])
        u = _dot(x, wub_ref[...])
        hmid = (a * _sigmoid(a) * u).astype(BF16)
        y_ref[...] = _dot(hmid, wdb_ref[...]).astype(BF16)

    @pl.when(i >= nu_ref[0])
    def _():
        y_ref[...] = jnp.zeros_like(y_ref)


def _experts(blk_e, n_used, xs, wg, wu, wd, layer):
    n_slots, d = xs.shape
    de = wg.shape[-1]
    grid_spec = pltpu.PrefetchScalarGridSpec(
        num_scalar_prefetch=2,
        grid=(n_slots // MOE_BLOCK,),
        in_specs=[
            pl.BlockSpec((MOE_BLOCK, d), lambda i, be, nu: (i, 0)),
            pl.BlockSpec((None, None, d, de), lambda i, be, nu: (layer, be[i], 0, 0)),
            pl.BlockSpec((None, None, d, de), lambda i, be, nu: (layer, be[i], 0, 0)),
            pl.BlockSpec((None, None, de, d), lambda i, be, nu: (layer, be[i], 0, 0)),
        ],
        out_specs=pl.BlockSpec((MOE_BLOCK, d), lambda i, be, nu: (i, 0)),
        scratch_shapes=[pltpu.VMEM((d, de), BF16), pltpu.VMEM((d, de), BF16), pltpu.VMEM((de, d), BF16)],
    )
    return pl.pallas_call(
        _expert_kernel,
        grid_spec=grid_spec,
        out_shape=jax.ShapeDtypeStruct((n_slots, d), BF16),
        compiler_params=_cparams("arbitrary"),
        name="moe_experts",
    )(blk_e, n_used, xs, wg, wu, wd)


def _combine_kernel(x_ref, y0_ref, y1_ref, rt_ref, mod_ref, o_ref):
    m = mod_ref[0]
    rt = rt_ref[...]
    y = rt[:, 2:3] * y0_ref[...].astype(F32) + rt[:, 3:4] * y1_ref[...].astype(F32)
    o_ref[...] = x_ref[...] + m[5:6] * y


def _combine(x2, y0, y1, rt, mod, tpb, bsz):
    n, d = x2.shape
    tile = lambda w: pl.BlockSpec((ROWS, w), lambda t: (t, 0))
    return pl.pallas_call(
        _combine_kernel,
        grid=(n // ROWS,),
        in_specs=[tile(d), tile(d), tile(d), tile(LANES), pl.BlockSpec((1, N_MOD, d), _mod_row(tpb, bsz))],
        out_specs=tile(d),
        out_shape=jax.ShapeDtypeStruct((n, d), F32),
        compiler_params=_cparams("parallel"),
        name="moe_combine",
    )(x2, y0, y1, rt, mod)


def _route_tables(rt, counts_f, n):
    counts = counts_f[0, N_GROUPS:N_GROUPS + N_EXPERTS].astype(jnp.int32)
    padded = (counts + MOE_BLOCK - 1) // MOE_BLOCK * MOE_BLOCK
    pad_end = jnp.cumsum(padded)
    pad_start = pad_end - padded
    eid = rt[:, 0:TOP_K].astype(jnp.int32)
    rank = rt[:, 4:4 + TOP_K].astype(jnp.int32)
    onehot = eid[:, :, None] == jnp.arange(N_EXPERTS, dtype=jnp.int32)
    slot_of = jnp.sum(jnp.where(onehot, pad_start, 0), axis=-1) + rank
    n_slots = -(-n * TOP_K // MOE_BLOCK) * MOE_BLOCK + N_EXPERTS * MOE_BLOCK
    blk_start = jnp.arange(n_slots // MOE_BLOCK, dtype=jnp.int32) * MOE_BLOCK
    blk_e = jnp.minimum(jnp.sum(blk_start[:, None] >= pad_end[None, :], axis=-1), N_EXPERTS - 1).astype(jnp.int32)
    n_used = (pad_end[-1] // MOE_BLOCK).astype(jnp.int32).reshape(1)
    return slot_of, blk_e, n_used, n_slots


def _final_kernel(x_ref, w_ref, o_ref):
    x = x_ref[...]
    o_ref[...] = x * lax.rsqrt(jnp.mean(x * x, axis=-1, keepdims=True) + EPS) * w_ref[...]


def _final_norm(x3, w, seq):
    bsz, _, d = x3.shape
    return pl.pallas_call(
        _final_kernel,
        grid=(bsz, seq // ROWS),
        in_specs=[pl.BlockSpec((None, ROWS, d), lambda b, i: (b, i, 0)), pl.BlockSpec((1, d), lambda b, i: (0, 0))],
        out_specs=pl.BlockSpec((None, ROWS, d), lambda b, i: (b, i, 0)),
        out_shape=jax.ShapeDtypeStruct((bsz, seq, d), F32),
        compiler_params=_cparams("parallel", "parallel"),
        name="final_norm",
    )(x3, w)


def _channel_dft(group_dim):
    k = (jnp.arange(group_dim, dtype=jnp.int32)[:, None] * jnp.arange(group_dim, dtype=jnp.int32)[None, :]) % group_dim
    ang = k.astype(F32) * (2.0 * jnp.pi / group_dim)
    eye = jnp.eye(FOURIER_GROUPS, dtype=F32)
    return jnp.kron(eye, jnp.cos(ang)), jnp.kron(eye, jnp.sin(ang))


def kernel(x, c, ctx, c_ctx, w_mod, b_mod, norm_mix, norm_ffn, w_in, conv_w, a_log, dt_bias, out_norm,
           w_fourier, w_delta, w_out, w_route_group, b_route_group, w_route_expert, b_route_expert,
           w_gate, w_up, w_down, final_norm):
    bsz, seq, d = x.shape
    ctx_len = ctx.shape[1]
    depth = w_mod.shape[0]
    assert d == HEADS * HEAD_DIM and ctx_len == ROWS and seq % ROWS == 0 and bsz < 16
    tt = seq + ctx_len
    tpb = tt // ROWS
    n = bsz * tt
    fw = w_fourier.shape[1]
    group_dim = fw // FOURIER_GROUPS

    cvec = jnp.zeros((16, d), F32).at[:bsz].set(c).at[bsz].set(c_ctx)
    mod_all = _modulation(cvec, w_mod, b_mod).reshape(depth, 16, N_MOD, d)
    cc, sc = _channel_dft(group_dim)

    o_a = 3 * DW
    o_b = o_a + 2 * HEADS
    o_z = o_b + 2 * HEADS
    o_f = o_z + DW
    o_g = o_f + fw

    xcur = jnp.concatenate([x, ctx], axis=1).reshape(n, d)
    for i in range(depth):
        wi = w_in[i]
        wfold_c = jnp.dot(wi[:, o_f:o_g], cc, precision=lax.Precision.HIGHEST)
        wfold_s = jnp.dot(wi[:, o_f:o_g], sc, precision=lax.Precision.HIGHEST)
        w_main = jnp.concatenate([wi[:, :o_a], wi[:, o_z:o_f], wi[:, o_g:], wfold_c, wfold_s], axis=1).astype(BF16)
        w_ab = jnp.concatenate([wi[:, o_a:o_z], jnp.zeros((d, LANES - 4 * HEADS), F32)], axis=1).astype(BF16)
        gate_params = jnp.zeros((8, LANES), F32)
        gate_params = gate_params.at[0, :2 * HEADS].set(a_log[i].reshape(-1)).at[1, :2 * HEADS].set(dt_bias[i].reshape(-1))
        mod = mod_all[i]

        p_main, ab = _inproj(xcur, mod, norm_mix[i].reshape(1, d), w_main, w_ab, tpb, bsz)
        u, w, kt, qd, qkd, gs = _prep(p_main, ab, conv_w[i], gate_params, tpb)
        of, ob = _scan(u, w, kt, qd, qkd, gs, bsz, seq // ROWS, ctx_len // ROWS)
        fm = _fourier(p_main.reshape(bsz, tt, MAIN_COLS), seq, ctx_len, group_dim).reshape(n, fw)

        wr = jnp.zeros((d, LANES), F32).at[:, :N_GROUPS].set(w_route_group[i])
        wr = wr.at[:, N_GROUPS:N_GROUPS + N_EXPERTS].set(w_route_expert[i])
        br = jnp.zeros((1, LANES), F32).at[0, :N_GROUPS].set(b_route_group[i])
        br = br.at[0, N_GROUPS:N_GROUPS + N_EXPERTS].set(b_route_expert[i])
        xcur, h2, rt, counts = _merge(of, ob, p_main, fm, xcur, mod, out_norm[i].reshape(1, HEAD_DIM),
                                      norm_ffn[i].reshape(1, d), w_fourier[i].astype(BF16),
                                      w_delta[i].astype(BF16), w_out[i].astype(BF16), wr, br, tpb, bsz)

        slot_of, blk_e, n_used, n_slots = _route_tables(rt, counts, n)
        slots = jnp.swapaxes(slot_of.reshape(n // ROWS, ROWS, TOP_K), 1, 2)
        xs = _dispatch(slots, h2, n_slots)
        ys = _experts(blk_e, n_used, xs, w_gate, w_up, w_down, i)
        xcur = _combine(xcur, ys[slot_of[:, 0]], ys[slot_of[:, 1]], rt, mod, tpb, bsz)

    return _final_norm(xcur.reshape(bsz, tt, d), final_norm.reshape(1, d), seq)
```
